```python
import math
import jax
import jax.numpy as jnp
from jax import lax
import numpy as np

D_MODEL = 1024
BATCH = 16
SEQ = 256
DEPTH = 4
DEC_BATCH = 8
DEC_SEQ = 2048
PAST_LEN = 256

GRID_W = 64
N_MIXERS = 3
N_CONV_LAYERS = (DEPTH + 2) // 3
N_GLA_LAYERS = (DEPTH + 1) // 3
N_NA_LAYERS = DEPTH // 3
CONV_WIDTH = 31
GLA_HEADS = 4
GLA_DK = D_MODEL // 2
GLA_DV = D_MODEL
GLA_DKH = GLA_DK // GLA_HEADS
GLA_DVH = GLA_DV // GLA_HEADS
GLA_GATE_RANK = 16
GLA_GATE_TAU = 16.0
GLA_CHUNK = 64
NA_HEADS = 16
NA_DH = D_MODEL // NA_HEADS
NA_WIN_R = 8
NA_WIN_C = 16
N_EXPERTS = 16
EXPERT_CAP_FACTOR = 2
D_EXPERT = 2 * D_MODEL
ROPE_THETA = 10000.0
EPS = 1e-6

kernel_name = 'hybrid_conv_gla_natten_ec_diffusion_step'

F32 = jnp.float32


def rmsnorm(x, g):
    xf = x.astype(F32)
    y = xf * lax.rsqrt(jnp.mean(xf * xf, axis=-1, keepdims=True) + EPS)
    return (y * g.astype(F32)).astype(x.dtype)


def layernorm(x, g, b):
    xf = x.astype(F32)
    mu = jnp.mean(xf, axis=-1, keepdims=True)
    xc = xf - mu
    y = xc * lax.rsqrt(jnp.mean(xc * xc, axis=-1, keepdims=True) + EPS)
    return (y * g.astype(F32) + b.astype(F32)).astype(x.dtype)


def modulate(h, shift, scale):
    return h * (1 + scale) + shift


def axial_rope(x):
    L, dh = x.shape[2], x.shape[-1]
    half = dh // 2
    nf = half // 2
    t = jnp.arange(L)
    row = (t // GRID_W).astype(F32)
    col = (t % GRID_W).astype(F32)
    inv = ROPE_THETA ** (-jnp.arange(nf, dtype=F32) / nf)
    xf = x.astype(F32)

    def rot(xp, pos):
        ang = pos[:, None] * inv[None, :]
        cos, sin = jnp.cos(ang), jnp.sin(ang)
        x1, x2 = xp[..., :nf], xp[..., nf:]
        return jnp.concatenate([x1 * cos - x2 * sin, x1 * sin + x2 * cos], axis=-1)

    out = jnp.concatenate([rot(xf[..., :half], row), rot(xf[..., half:], col)], axis=-1)
    return out.astype(x.dtype)


def conv_module(h, w_in, b_in, dw, dw_b, ln_g, ln_b, w_out, b_out):
    a, gate = jnp.split(h @ w_in + b_in, 2, axis=-1)
    u = a * jax.nn.sigmoid(gate)
    pad = CONV_WIDTH // 2
    u = lax.conv_general_dilated(u, dw[:, None, :].astype(u.dtype), (1,), [(pad, pad)],
                                 dimension_numbers=('NWC', 'WIO', 'NWC'),
                                 feature_group_count=D_MODEL) + dw_b
    u = jax.nn.silu(layernorm(u, ln_g, ln_b))
    return u @ w_out + b_out


def gla_chunked(q, k, v, logg, s0):
    B, H, L, dk = q.shape
    dv = v.shape[-1]
    C = GLA_CHUNK
    n = L // C
    rs = lambda a: a.astype(F32).reshape(B, H, n, C, a.shape[-1])
    q, k, v, logg = rs(q), rs(k), rs(v), rs(logg)
    b = jnp.cumsum(logg, axis=3)
    b_last = b[:, :, :, -1:, :]
    qe = q * jnp.exp(b)
    ke = k * jnp.exp(-b)
    kd = k * jnp.exp(b_last - b)
    causal = jnp.tril(jnp.ones((C, C), dtype=bool))
    a = jnp.where(causal, jnp.einsum('bhntd,bhnsd->bhnts', qe, ke), 0.0)
    o_intra = jnp.einsum('bhnts,bhnse->bhnte', a, v)
    ds = jnp.einsum('bhnsd,bhnse->nbhde', kd, v)
    decay = jnp.moveaxis(jnp.exp(b_last[:, :, :, 0, :]), 2, 0)

    def step(s, inp):
        dec, d = inp
        return dec[..., None] * s + d, s

    s_fin, s_in = lax.scan(step, s0.astype(F32), (decay, ds))
    o_inter = jnp.einsum('bhntd,nbhde->bhnte', qe, s_in)
    return (o_intra + o_inter).reshape(B, H, L, dv), s_fin


def gla_mixer(h, s0, use_rope, w_proj, w_a1, w_a2, b_a, norm_g, w_o):
    B, L, _ = h.shape
    q, k, v, g = jnp.split(h @ w_proj, [GLA_DK, 2 * GLA_DK, 2 * GLA_DK + GLA_DV], axis=-1)
    heads = lambda a, d: a.reshape(B, L, GLA_HEADS, d).transpose(0, 2, 1, 3)
    q, k, v = heads(q, GLA_DKH), heads(k, GLA_DKH), heads(v, GLA_DVH)
    if use_rope:
        q, k = axial_rope(q), axial_rope(k)
    q = q * (GLA_DKH ** -0.5)
    z = jnp.einsum('bld,rdk->rblk', h, w_a1)
    z = jnp.einsum('rblk,rke->rble', z, w_a2) + b_a[:, None, None, :]
    logg = jax.nn.log_sigmoid(z.astype(F32)) / GLA_GATE_TAU
    logg = logg.reshape(2, B, L, GLA_HEADS, GLA_DKH).transpose(0, 1, 3, 2, 4)
    o_f, s_f = gla_chunked(q, k, v, logg[0], s0[:, 0])
    flip = lambda a: jnp.flip(a, axis=2)
    o_b, s_b = gla_chunked(flip(q), flip(k), flip(v), flip(logg[1]), s0[:, 1])
    o = (o_f + flip(o_b)).transpose(0, 2, 1, 3)
    o = o * lax.rsqrt(jnp.mean(o * o, axis=-1, keepdims=True) + EPS)
    o = o * norm_g.astype(F32).reshape(GLA_HEADS, GLA_DVH)
    o = o.reshape(B, L, GLA_DV).astype(h.dtype) * jax.nn.silu(g)
    return o @ w_o, jnp.stack([s_f, s_b], axis=1).astype(h.dtype)


def na_context(h, w_qkv, w_o):
    B, L, _ = h.shape
    q, k, v = [a.reshape(B, L, NA_HEADS, NA_DH) for a in jnp.split(h @ w_qkv, 3, axis=-1)]
    s = jnp.einsum('bqhd,bkhd->bhqk', q, k).astype(F32) * (NA_DH ** -0.5)
    p = jax.nn.softmax(s, axis=-1).astype(v.dtype)
    o = jnp.einsum('bhqk,bkhd->bqhd', p, v).reshape(B, L, D_MODEL)
    return o @ w_o, k, v


def na_latent(h, k_ctx, v_ctx, w_qkv, rpb, w_o):
    B, L, _ = h.shape
    rows = L // GRID_W
    wr = min(NA_WIN_R, rows)
    q, k, v = [a.reshape(B, rows, GRID_W, NA_HEADS, NA_DH) for a in jnp.split(h @ w_qkv, 3, axis=-1)]
    r = jnp.arange(rows)
    r_start = jnp.clip(r - wr // 2, 0, rows - wr)
    row_idx = r_start[:, None] + jnp.arange(wr)
    kw = k[:, row_idx]
    vw = v[:, row_idx].reshape(B, rows, wr * GRID_W, NA_HEADS, NA_DH)
    scale = NA_DH ** -0.5
    s_loc = jnp.einsum('brqhd,brikhd->brhqik', q, kw).astype(F32) * scale
    cidx = jnp.arange(GRID_W)
    c_start = jnp.clip(cidx - NA_WIN_C // 2, 0, GRID_W - NA_WIN_C)
    col_ok = (cidx[None, :] >= c_start[:, None]) & (cidx[None, :] < c_start[:, None] + NA_WIN_C)
    dr = row_idx - r[:, None] + (NA_WIN_R - 1)
    dc = jnp.clip(cidx[None, :] - cidx[:, None] + (NA_WIN_C - 1), 0, 2 * NA_WIN_C - 2)
    bias = rpb[:, dr[:, :, None, None], dc[None, None, :, :]]
    bias = bias.transpose(1, 0, 3, 2, 4).astype(F32)
    s_loc = jnp.where(col_ok[:, None, :], s_loc + bias[None], -jnp.inf)
    s_loc = s_loc.reshape(B, rows, NA_HEADS, GRID_W, wr * GRID_W)
    s_ctx = jnp.einsum('brqhd,bchd->brhqc', q, k_ctx).astype(F32) * scale
    p = jax.nn.softmax(jnp.concatenate([s_loc, s_ctx], axis=-1), axis=-1).astype(v.dtype)
    n_loc = wr * GRID_W
    o = (jnp.einsum('brhqn,brnhd->brqhd', p[..., :n_loc], vw)
         + jnp.einsum('brhqc,bchd->brqhd', p[..., n_loc:], v_ctx))
    return o.reshape(B, L, D_MODEL) @ w_o


def ec_moe(h, router_w, w_gate, w_up, w_down):
    B, L, _ = h.shape
    cap = EXPERT_CAP_FACTOR * L // N_EXPERTS
    aff = jax.nn.softmax((h @ router_w).astype(F32), axis=-1)
    top_val, top_idx = lax.top_k(aff.transpose(0, 2, 1), cap)
    bidx = jnp.arange(B)[:, None, None]
    xs = h[bidx, top_idx]
    hid = jax.nn.silu(jnp.einsum('becd,edf->becf', xs, w_gate)) * jnp.einsum('becd,edf->becf', xs, w_up)
    out = jnp.einsum('becf,efd->becd', hid, w_down) * top_val[..., None].astype(h.dtype)
    return jnp.zeros_like(h).at[bidx, top_idx].add(out)


def setup_inputs(seed: int = 0) -> dict:
    key = jax.random.key(seed)
    ks = iter(jax.random.split(key, 40))
    nrm = lambda shape, s: jax.random.normal(next(ks), shape, F32) * s
    D, E, F = D_MODEL, N_EXPERTS, D_EXPERT
    return {
        'x_prompt': nrm((BATCH, SEQ, D), 1.0),
        'x_sample': nrm((DEC_BATCH, DEC_SEQ, D), 1.0),
        'state_gla': nrm((DEC_BATCH, N_GLA_LAYERS, 2, GLA_HEADS, GLA_DKH, GLA_DVH), 0.5),
        'cache_na_k': nrm((DEC_BATCH, N_NA_LAYERS, PAST_LEN, NA_HEADS, NA_DH), 1.0),
        'cache_na_v': nrm((DEC_BATCH, N_NA_LAYERS, PAST_LEN, NA_HEADS, NA_DH), 1.0),
        'c': nrm((DEC_BATCH, D), 1.0),
        'c_ctx': nrm((D,), 1.0),
        'norm1_g': 1.0 + nrm((DEPTH, D), 0.1),
        'norm2_g': 1.0 + nrm((DEPTH, D), 0.1),
        'w_mod': nrm((DEPTH, D, 6 * D), 0.5 * D ** -0.5),
        'b_mod': nrm((DEPTH, 6 * D), 0.05),
        'conv_w_in': nrm((N_CONV_LAYERS, D, 2 * D), D ** -0.5),
        'conv_b_in': nrm((N_CONV_LAYERS, 2 * D), 0.02),
        'conv_dw': nrm((N_CONV_LAYERS, CONV_WIDTH, D), CONV_WIDTH ** -0.5),
        'conv_dw_b': nrm((N_CONV_LAYERS, D), 0.02),
        'conv_ln_g': 1.0 + nrm((N_CONV_LAYERS, D), 0.1),
        'conv_ln_b': nrm((N_CONV_LAYERS, D), 0.02),
        'conv_w_out': nrm((N_CONV_LAYERS, D, D), D ** -0.5),
        'conv_b_out': nrm((N_CONV_LAYERS, D), 0.02),
        'gla_w_proj': nrm((N_GLA_LAYERS, D, 2 * GLA_DK + 2 * GLA_DV), D ** -0.5),
        'gla_w_a1': nrm((N_GLA_LAYERS, 2, D, GLA_GATE_RANK), D ** -0.5),
        'gla_w_a2': nrm((N_GLA_LAYERS, 2, GLA_GATE_RANK, GLA_DK), GLA_GATE_RANK ** -0.5),
        'gla_b_a': nrm((N_GLA_LAYERS, 2, GLA_DK), 0.1),
        'gla_norm_g': 1.0 + nrm((N_GLA_LAYERS, GLA_DV), 0.1),
        'gla_w_o': nrm((N_GLA_LAYERS, GLA_DV, D), GLA_DV ** -0.5),
        'na_w_qkv': nrm((N_NA_LAYERS, D, 3 * D), D ** -0.5),
        'na_rpb': nrm((N_NA_LAYERS, NA_HEADS, 2 * NA_WIN_R - 1, 2 * NA_WIN_C - 1), 0.2),
        'na_w_o': nrm((N_NA_LAYERS, D, D), D ** -0.5),
        'router_w': nrm((DEPTH, D, E), D ** -0.5),
        'moe_w_gate': nrm((DEPTH, E, D, F), D ** -0.5),
        'moe_w_up': nrm((DEPTH, E, D, F), D ** -0.5),
        'moe_w_down': nrm((DEPTH, E, F, D), F ** -0.5),
        'final_g': 1.0 + nrm((D,), 0.1),
    }


def reference(x_prompt, x_sample, state_gla, cache_na_k, cache_na_v, c, c_ctx,
              norm1_g, norm2_g, w_mod, b_mod,
              conv_w_in, conv_b_in, conv_dw, conv_dw_b, conv_ln_g, conv_ln_b, conv_w_out, conv_b_out,
              gla_w_proj, gla_w_a1, gla_w_a2, gla_b_a, gla_norm_g, gla_w_o,
              na_w_qkv, na_rpb, na_w_o,
              router_w, moe_w_gate, moe_w_up, moe_w_down, final_g):
    xc, xs = x_prompt, x_sample
    new_gla, new_k, new_v = [], [], []
    for i in range(DEPTH):
        kind, j = i % N_MIXERS, i // N_MIXERS
        mc = jnp.split(jax.nn.silu(c_ctx) @ w_mod[i] + b_mod[i], 6, axis=-1)
        mx = jnp.split((jax.nn.silu(c) @ w_mod[i] + b_mod[i])[:, None, :], 6, axis=-1)
        hc = modulate(rmsnorm(xc, norm1_g[i]), mc[0], mc[1])
        hx = modulate(rmsnorm(xs, norm1_g[i]), mx[0], mx[1])
        if kind == 0:
            p = (conv_w_in[j], conv_b_in[j], conv_dw[j], conv_dw_b[j], conv_ln_g[j], conv_ln_b[j],
                 conv_w_out[j], conv_b_out[j])
            oc = conv_module(hc, *p)
            ox = conv_module(hx, *p)
        elif kind == 1:
            p = (gla_w_proj[j], gla_w_a1[j], gla_w_a2[j], gla_b_a[j], gla_norm_g[j], gla_w_o[j])
            s_zero = jnp.zeros((xc.shape[0], 2, GLA_HEADS, GLA_DKH, GLA_DVH), xc.dtype)
            oc, s_ctx = gla_mixer(hc, s_zero, False, *p)
            ox, _ = gla_mixer(hx, state_gla[:, j], True, *p)
            new_gla.append(s_ctx)
        else:
            oc, k_c, v_c = na_context(hc, na_w_qkv[j], na_w_o[j])
            ox = na_latent(hx, cache_na_k[:, j], cache_na_v[:, j], na_w_qkv[j], na_rpb[j], na_w_o[j])
            new_k.append(k_c)
            new_v.append(v_c)
        xc = xc + mc[2] * oc
        xs = xs + mx[2] * ox
        hc = modulate(rmsnorm(xc, norm2_g[i]), mc[3], mc[4])
        hx = modulate(rmsnorm(xs, norm2_g[i]), mx[3], mx[4])
        xc = xc + mc[5] * ec_moe(hc, router_w[i], moe_w_gate[i], moe_w_up[i], moe_w_down[i])
        xs = xs + mx[5] * ec_moe(hx, router_w[i], moe_w_gate[i], moe_w_up[i], moe_w_down[i])
    y_prompt = rmsnorm(xc, final_g)
    y_sample = rmsnorm(xs, final_g)
    new_state_gla = jnp.stack(new_gla, axis=1)
    new_cache_na_k = jnp.stack(new_k, axis=1)
    new_cache_na_v = jnp.stack(new_v, axis=1)
    return (y_prompt, y_sample, new_state_gla, new_cache_na_k, new_cache_na_v)
```

```python
import functools
import math

import jax
import jax.numpy as jnp
from jax import lax
from jax.experimental import pallas as pl
from jax.experimental.pallas import tpu as pltpu

F32 = jnp.float32
BF16 = jnp.bfloat16
I32 = jnp.int32

D = 1024
DEPTH = 4
GRID_W = 64
N_MIXERS = 3
CONV_WIDTH = 31
CONV_PAD = CONV_WIDTH // 2
GLA_HEADS = 4
GLA_DK = D // 2
GLA_DV = D
GLA_DKH = GLA_DK // GLA_HEADS
GLA_DVH = GLA_DV // GLA_HEADS
GLA_GATE_RANK = 16
GLA_GATE_TAU = 16.0
GLA_CHUNK = 64
NA_HEADS = 16
NA_DH = D // NA_HEADS
NA_WIN_R = 8
NA_WIN_C = 16
N_EXPERTS = 16
EXPERT_CAP_FACTOR = 2
D_EXPERT = 2 * D
ROPE_THETA = 10000.0
EPS = 1e-6

TM = 256
HALO = 16
F_TILE = 512
FFN_ROWS = 512
VMEM_LIMIT = 56 * 1024 * 1024


def _cparams(*sem):
    return pltpu.CompilerParams(dimension_semantics=sem, vmem_limit_bytes=VMEM_LIMIT)


def _norm_mod(x, g, shift, scale):
    ms = jnp.mean(x * x, axis=-1, keepdims=True)
    y = x * lax.rsqrt(ms + EPS) * g
    return y * (1.0 + scale) + shift


def _mod_index(tiles_per_req, shared):
    if shared:
        return lambda i: (0, 0, 0)
    return lambda i: (i // tiles_per_req, 0, 0)


def _mod_kernel(c_ref, w_ref, b_ref, o_ref):
    c = c_ref[...]
    a = (c * jax.nn.sigmoid(c)).astype(BF16)
    o_ref[0] = jnp.dot(a, w_ref[0].astype(BF16), preferred_element_type=F32) + b_ref[0]


def _mod_vectors(cvec, w_mod, b_mod):
    rows = cvec.shape[0]
    n = w_mod.shape[-1]
    tn = D
    return pl.pallas_call(
        _mod_kernel,
        out_shape=jax.ShapeDtypeStruct((DEPTH, rows, n), F32),
        grid=(DEPTH, n // tn),
        in_specs=[
            pl.BlockSpec((rows, D), lambda l, j: (0, 0)),
            pl.BlockSpec((1, D, tn), lambda l, j: (l, 0, j)),
            pl.BlockSpec((1, 1, tn), lambda l, j: (l, 0, j)),
        ],
        out_specs=pl.BlockSpec((1, rows, tn), lambda l, j: (l, 0, j)),
        compiler_params=_cparams("arbitrary", "arbitrary"),
        name="mod_vectors",
    )(cvec, w_mod, b_mod.reshape(DEPTH, 1, n))


def _conv_in_kernel(x_ref, g_ref, mod_ref, w_ref, b_ref, u_ref):
    h = _norm_mod(x_ref[...], g_ref[...], mod_ref[0, 0:1, :], mod_ref[0, 1:2, :])
    ag = jnp.dot(h.astype(BF16), w_ref[...], preferred_element_type=F32) + b_ref[...]
    u_ref[...] = ag[:, :D] * jax.nn.sigmoid(ag[:, D:])


def _conv_in(x, g, mod, w_in, b_in, tiles_per_req, shared):
    t = x.shape[0]
    return pl.pallas_call(
        _conv_in_kernel,
        out_shape=jax.ShapeDtypeStruct((t, D), F32),
        grid=(t // TM,),
        in_specs=[
            pl.BlockSpec((TM, D), lambda i: (i, 0)),
            pl.BlockSpec((1, D), lambda i: (0, 0)),
            pl.BlockSpec((1, 8, D), _mod_index(tiles_per_req, shared)),
            pl.BlockSpec((D, 2 * D), lambda i: (0, 0)),
            pl.BlockSpec((1, 2 * D), lambda i: (0, 0)),
        ],
        out_specs=pl.BlockSpec((TM, D), lambda i: (i, 0)),
        compiler_params=_cparams("arbitrary"),
        name="conv_in",
    )(x, g, mod, w_in, b_in)


def _conv_out_kernel(tiles_per_req, x_ref, u_ref, up_ref, un_ref, mod_ref, dw_ref, dwb_ref,
                     lng_ref, lnb_ref, w_ref, b_ref, o_ref, win_ref):
    k = pl.program_id(0) % tiles_per_req
    win_ref[0:HALO, :] = jnp.where(k > 0, up_ref[...], 0.0)
    win_ref[HALO:HALO + TM, :] = u_ref[...]
    win_ref[HALO + TM:, :] = jnp.where(k < tiles_per_req - 1, un_ref[...], 0.0)
    rows = 32
    base = HALO - CONV_PAD
    for c in range(TM // rows):
        acc = jnp.zeros((rows, D), F32)
        for tap in range(CONV_WIDTH):
            s = c * rows + base + tap
            acc = acc + dw_ref[tap:tap + 1, :] * win_ref[s:s + rows, :]
        v = acc + dwb_ref[...]
        mu = jnp.mean(v, axis=-1, keepdims=True)
        vc = v - mu
        y = vc * lax.rsqrt(jnp.mean(vc * vc, axis=-1, keepdims=True) + EPS)
        y = y * lng_ref[...] + lnb_ref[...]
        y = y * jax.nn.sigmoid(y)
        res = jnp.dot(y.astype(BF16), w_ref[...], preferred_element_type=F32) + b_ref[...]
        r0 = c * rows
        o_ref[r0:r0 + rows, :] = x_ref[r0:r0 + rows, :] + mod_ref[0, 2:3, :] * res


def _conv_out(x, u, mod, dw, dw_b, ln_g, ln_b, w_out, b_out, tiles_per_req, shared):
    t = x.shape[0]
    hb = TM // HALO
    n_halo = t // HALO
    vec = pl.BlockSpec((1, D), lambda i: (0, 0))
    return pl.pallas_call(
        functools.partial(_conv_out_kernel, tiles_per_req),
        out_shape=jax.ShapeDtypeStruct((t, D), F32),
        grid=(t // TM,),
        in_specs=[
            pl.BlockSpec((TM, D), lambda i: (i, 0)),
            pl.BlockSpec((TM, D), lambda i: (i, 0)),
            pl.BlockSpec((HALO, D), lambda i: (jnp.maximum(i * hb - 1, 0), 0)),
            pl.BlockSpec((HALO, D), lambda i: (jnp.minimum((i + 1) * hb, n_halo - 1), 0)),
            pl.BlockSpec((1, 8, D), _mod_index(tiles_per_req, shared)),
            pl.BlockSpec((CONV_WIDTH, D), lambda i: (0, 0)),
            vec, vec, vec,
            pl.BlockSpec((D, D), lambda i: (0, 0)),
            vec,
        ],
        out_specs=pl.BlockSpec((TM, D), lambda i: (i, 0)),
        scratch_shapes=[pltpu.VMEM((TM + 2 * HALO, D), F32)],
        compiler_params=_cparams("arbitrary"),
        name="conv_out",
    )(x, u, u, u, mod, dw, dw_b, ln_g, ln_b, w_out, b_out)


def _split_bf16(a):
    hi = a.astype(BF16)
    lo = (a - hi.astype(F32)).astype(BF16)
    return hi, lo


def _moe_pre_kernel(x_ref, g_ref, mod_ref, rw_ref, h_ref, aff_ref):
    h = _norm_mod(x_ref[...], g_ref[...], mod_ref[0, 3:4, :], mod_ref[0, 4:5, :])
    h_ref[...] = h.astype(BF16)
    h_hi, h_lo = _split_bf16(h)
    r_hi, r_lo = _split_bf16(rw_ref[...])
    nt = (((1,), (1,)), ((), ()))
    logits = (lax.dot_general(r_hi, h_hi, nt, preferred_element_type=F32)
              + lax.dot_general(r_hi, h_lo, nt, preferred_element_type=F32)
              + lax.dot_general(r_lo, h_hi, nt, preferred_element_type=F32))
    m = jnp.max(logits, axis=0, keepdims=True)
    e = jnp.exp(logits - m)
    aff_ref[0] = e / jnp.sum(e, axis=0, keepdims=True)


def _moe_pre(x, g, mod, router_wt, n_req, tiles_per_req, shared):
    t = x.shape[0]
    seq = tiles_per_req * TM
    return pl.pallas_call(
        _moe_pre_kernel,
        out_shape=(jax.ShapeDtypeStruct((t, D), BF16),
                   jax.ShapeDtypeStruct((n_req, N_EXPERTS, seq), F32)),
        grid=(t // TM,),
        in_specs=[
            pl.BlockSpec((TM, D), lambda i: (i, 0)),
            pl.BlockSpec((1, D), lambda i: (0, 0)),
            pl.BlockSpec((1, 8, D), _mod_index(tiles_per_req, shared)),
            pl.BlockSpec((N_EXPERTS, D), lambda i: (0, 0)),
        ],
        out_specs=(pl.BlockSpec((TM, D), lambda i: (i, 0)),
                   pl.BlockSpec((1, N_EXPERTS, TM), lambda i: (i // tiles_per_req, 0, i % tiles_per_req))),
        compiler_params=_cparams("arbitrary"),
        name="moe_pre",
    )(x, g, mod, router_wt)


def _route_kernel(cap, aff_ref, pos_ref):
    aff = aff_ref[...]
    nb, ne, seq = aff.shape
    aff = aff.reshape(nb * ne, seq)
    rows = nb * ne
    bits = pltpu.bitcast(aff, I32)
    thr = jnp.zeros((rows, 1), I32)
    for bit in range(30, -1, -1):
        cand = thr | (1 << bit)
        cnt = jnp.sum(jnp.where(bits >= cand, 1.0, 0.0), axis=1, keepdims=True)
        thr = jnp.where(cnt >= cap, cand, thr)
    gt = bits > thr
    eq = bits == thr
    need = cap - jnp.sum(jnp.where(gt, 1.0, 0.0), axis=1, keepdims=True)
    blk = 256
    r_i = lax.broadcasted_iota(I32, (blk, blk), 0)
    c_i = lax.broadcasted_iota(I32, (blk, blk), 1)
    upper = jnp.where(r_i < c_i, 1.0, 0.0).astype(BF16)
    off_gt = jnp.zeros((rows, 1), F32)
    off_eq = jnp.zeros((rows, 1), F32)
    for c in range(seq // blk):
        sl = slice(c * blk, (c + 1) * blk)
        g = jnp.where(gt[:, sl], 1.0, 0.0)
        q = jnp.where(eq[:, sl], 1.0, 0.0)
        cs_gt = jnp.dot(g.astype(BF16), upper, preferred_element_type=F32) + off_gt
        cs_eq = jnp.dot(q.astype(BF16), upper, preferred_element_type=F32) + off_eq
        off_gt = off_gt + jnp.sum(g, axis=1, keepdims=True)
        off_eq = off_eq + jnp.sum(q, axis=1, keepdims=True)
        sel = (g > 0.5) | ((q > 0.5) & (cs_eq < need))
        pos = cs_gt + jnp.minimum(cs_eq, need)
        pos_ref[:, :, sl] = jnp.where(sel, pos, -1.0).astype(I32).reshape(nb, ne, blk)


def _route(aff, cap, req_per_step):
    n_req, ne, seq = aff.shape
    return pl.pallas_call(
        functools.partial(_route_kernel, cap),
        out_shape=jax.ShapeDtypeStruct((n_req, ne, seq), I32),
        grid=(n_req // req_per_step,),
        in_specs=[pl.BlockSpec((req_per_step, ne, seq), lambda b: (b, 0, 0))],
        out_specs=pl.BlockSpec((req_per_step, ne, seq), lambda b: (b, 0, 0)),
        compiler_params=_cparams("arbitrary"),
        name="moe_route",
    )(aff)


def _gather_kernel(cap, experts_per_step, pos_ref, aff_ref, h_ref, xg_ref, w_ref):
    seq = h_ref.shape[0]
    h = h_ref[...]
    slot = lax.broadcasted_iota(I32, (cap, seq), 0)
    for k in range(experts_per_step):
        e = pl.program_id(1) * experts_per_step + k
        pos = pos_ref[0, pl.ds(e, 1), :]
        aff = aff_ref[0, pl.ds(e, 1), :]
        hit = pos == slot
        onehot = jnp.where(hit, 1.0, 0.0).astype(BF16)
        xg_ref[k] = jnp.dot(onehot, h, preferred_element_type=F32).astype(BF16)
        w_ref[k] = jnp.sum(jnp.where(hit, aff, 0.0), axis=1, keepdims=True)


def _gather(pos, aff, h, cap, experts_per_step):
    n_req, ne, seq = pos.shape
    steps = ne // experts_per_step
    return pl.pallas_call(
        functools.partial(_gather_kernel, cap, experts_per_step),
        out_shape=(jax.ShapeDtypeStruct((ne, n_req * cap, D), BF16),
                   jax.ShapeDtypeStruct((ne, n_req * cap, 1), F32)),
        grid=(n_req, steps),
        in_specs=[
            pl.BlockSpec((1, ne, seq), lambda b, s: (b, 0, 0)),
            pl.BlockSpec((1, ne, seq), lambda b, s: (b, 0, 0)),
            pl.BlockSpec((seq, D), lambda b, s: (b, 0)),
        ],
        out_specs=(pl.BlockSpec((experts_per_step, cap, D), lambda b, s: (s, b, 0)),
                   pl.BlockSpec((experts_per_step, cap, 1), lambda b, s: (s, b, 0))),
        compiler_params=_cparams("arbitrary", "arbitrary"),
        name="moe_gather",
    )(pos, aff, h)


def _ffn_kernel(xp_ref, xs_ref, wp_ref, ws_ref, wg_ref, wu_ref, wd_ref, yp_ref, ys_ref,
                acc_ref, wgb_ref, wub_ref, wdb_ref):
    f = pl.program_id(1)
    last = pl.num_programs(1) - 1
    wgb_ref[...] = wg_ref[0].astype(BF16)
    wub_ref[...] = wu_ref[0].astype(BF16)
    wdb_ref[...] = wd_ref[0].astype(BF16)
    n_p = xp_ref.shape[1]
    n_s = xs_ref.shape[1]
    for x_ref, w_ref, y_ref, base, n in ((xp_ref, wp_ref, yp_ref, 0, n_p), (xs_ref, ws_ref, ys_ref, n_p, n_s)):
        for r0 in range(0, n, FFN_ROWS):
            x = x_ref[0, r0:r0 + FFN_ROWS, :]
            g = jnp.dot(x, wgb_ref[...], preferred_element_type=F32)
            u = jnp.dot(x, wub_ref[...], preferred_element_type=F32)
            hid = (g * jax.nn.sigmoid(g) * u).astype(BF16)
            part = jnp.dot(hid, wdb_ref[...], preferred_element_type=F32)
            rows = slice(base + r0, base + r0 + FFN_ROWS)

            @pl.when(f == 0)
            def _():
                acc_ref[rows, :] = part

            @pl.when(f > 0)
            def _():
                acc_ref[rows, :] = acc_ref[rows, :] + part

            @pl.when(f == last)
            def _():
                y_ref[0, r0:r0 + FFN_ROWS, :] = (acc_ref[rows, :] * w_ref[0, r0:r0 + FFN_ROWS, :]).astype(BF16)


def _ffn(xg_p, xg_s, w_p, w_s, w_gate, w_up, w_down):
    ne, n_p, _ = xg_p.shape
    n_s = xg_s.shape[1]
    fdim = w_gate.shape[-1]
    row = lambda n, last: pl.BlockSpec((1, n, last), lambda e, f: (e, 0, 0))
    return pl.pallas_call(
        _ffn_kernel,
        out_shape=(jax.ShapeDtypeStruct((ne, n_p, D), BF16), jax.ShapeDtypeStruct((ne, n_s, D), BF16)),
        grid=(ne, fdim // F_TILE),
        in_specs=[
            row(n_p, D), row(n_s, D), row(n_p, 1), row(n_s, 1),
            pl.BlockSpec((1, D, F_TILE), lambda e, f: (e, 0, f)),
            pl.BlockSpec((1, D, F_TILE), lambda e, f: (e, 0, f)),
            pl.BlockSpec((1, F_TILE, D), lambda e, f: (e, f, 0)),
        ],
        out_specs=(row(n_p, D), row(n_s, D)),
        scratch_shapes=[
            pltpu.VMEM((n_p + n_s, D), F32),
            pltpu.VMEM((D, F_TILE), BF16),
            pltpu.VMEM((D, F_TILE), BF16),
            pltpu.VMEM((F_TILE, D), BF16),
        ],
        compiler_params=_cparams("arbitrary", "arbitrary"),
        name="moe_ffn",
    )(xg_p, xg_s, w_p, w_s, w_gate, w_up, w_down)


def _combine_kernel(cap, x_ref, mod_ref, post_ref, y_ref, o_ref):
    ne = y_ref.shape[0]
    post = post_ref[...]
    lane = lax.broadcasted_iota(I32, (TM, ne * cap), 1)
    hit = jnp.zeros((TM, ne * cap), F32)
    for e in range(ne):
        col = post[:, e:e + 1]
        hit = hit + jnp.where((col >= 0) & (col + e * cap == lane), 1.0, 0.0)
    y = y_ref[...].reshape(ne * cap, D)
    res = jnp.dot(hit.astype(BF16), y, preferred_element_type=F32)
    o_ref[...] = x_ref[...] + mod_ref[0, 5:6, :] * res


def _combine(x, mod, pos_t, y, cap, tiles_per_req, shared):
    t = x.shape[0]
    ne = y.shape[0]
    return pl.pallas_call(
        functools.partial(_combine_kernel, cap),
        out_shape=jax.ShapeDtypeStruct((t, D), F32),
        grid=(t // TM,),
        in_specs=[
            pl.BlockSpec((TM, D), lambda i: (i, 0)),
            pl.BlockSpec((1, 8, D), _mod_index(tiles_per_req, shared)),
            pl.BlockSpec((TM, ne), lambda i: (i, 0)),
            pl.BlockSpec((ne, cap, D), lambda i: (0, i // tiles_per_req, 0)),
        ],
        out_specs=pl.BlockSpec((TM, D), lambda i: (i, 0)),
        compiler_params=_cparams("arbitrary"),
        name="moe_combine",
    )(x, mod, pos_t, y)


def _moe(groups, g2, router_wt, w_gate, w_up, w_down):
    staged = []
    for gr in groups:
        seq = gr["tpr"] * TM
        cap = EXPERT_CAP_FACTOR * seq // N_EXPERTS
        h, aff = _moe_pre(gr["x"], g2, gr["mod"], router_wt, gr["n_req"], gr["tpr"], gr["shared"])
        pos = _route(aff, cap, gr["route_batch"])
        xg, w = _gather(pos, aff, h, cap, gr["gather_experts"])
        pos_t = jnp.transpose(pos, (0, 2, 1)).reshape(gr["n_req"] * seq, N_EXPERTS)
        staged.append((cap, pos_t, xg, w))
    (cap_p, pos_p, xg_p, w_p), (cap_s, pos_s, xg_s, w_s) = staged
    y_p, y_s = _ffn(xg_p, xg_s, w_p, w_s, w_gate, w_up, w_down)
    outs = []
    for gr, (cap, pos_t, _, _), y in zip(groups, staged, (y_p, y_s)):
        outs.append(_combine(gr["x"], gr["mod"], pos_t, y, cap, gr["tpr"], gr["shared"]))
    return outs


def _rmsnorm(x, g):
    xf = x.astype(F32)
    y = xf * lax.rsqrt(jnp.mean(xf * xf, axis=-1, keepdims=True) + EPS)
    return (y * g.astype(F32)).astype(x.dtype)


def _axial_rope(x):
    L, dh = x.shape[2], x.shape[-1]
    half = dh // 2
    nf = half // 2
    t = jnp.arange(L)
    row = (t // GRID_W).astype(F32)
    col = (t % GRID_W).astype(F32)
    inv = ROPE_THETA ** (-jnp.arange(nf, dtype=F32) / nf)
    xf = x.astype(F32)

    def rot(xp, pos):
        ang = pos[:, None] * inv[None, :]
        cos, sin = jnp.cos(ang), jnp.sin(ang)
        x1, x2 = xp[..., :nf], xp[..., nf:]
        return jnp.concatenate([x1 * cos - x2 * sin, x1 * sin + x2 * cos], axis=-1)

    out = jnp.concatenate([rot(xf[..., :half], row), rot(xf[..., half:], col)], axis=-1)
    return out.astype(x.dtype)


def _gla_chunked(q, k, v, logg, s0):
    B, H, L, dk = q.shape
    dv = v.shape[-1]
    C = GLA_CHUNK
    n = L // C
    rs = lambda a: a.astype(F32).reshape(B, H, n, C, a.shape[-1])
    q, k, v, logg = rs(q), rs(k), rs(v), rs(logg)
    b = jnp.cumsum(logg, axis=3)
    b_last = b[:, :, :, -1:, :]
    qe = q * jnp.exp(b)
    ke = k * jnp.exp(-b)
    kd = k * jnp.exp(b_last - b)
    causal = jnp.tril(jnp.ones((C, C), dtype=bool))
    a = jnp.where(causal, jnp.einsum('bhntd,bhnsd->bhnts', qe, ke), 0.0)
    o_intra = jnp.einsum('bhnts,bhnse->bhnte', a, v)
    ds = jnp.einsum('bhnsd,bhnse->nbhde', kd, v)
    decay = jnp.moveaxis(jnp.exp(b_last[:, :, :, 0, :]), 2, 0)

    def step(s, inp):
        dec, d = inp
        return dec[..., None] * s + d, s

    s_fin, s_in = lax.scan(step, s0.astype(F32), (decay, ds))
    o_inter = jnp.einsum('bhntd,nbhde->bhnte', qe, s_in)
    return (o_intra + o_inter).reshape(B, H, L, dv), s_fin


def _gla_mixer(h, s0, use_rope, w_proj, w_a1, w_a2, b_a, norm_g, w_o):
    B, L, _ = h.shape
    q, k, v, g = jnp.split(h @ w_proj, [GLA_DK, 2 * GLA_DK, 2 * GLA_DK + GLA_DV], axis=-1)
    heads = lambda a, d: a.reshape(B, L, GLA_HEADS, d).transpose(0, 2, 1, 3)
    q, k, v = heads(q, GLA_DKH), heads(k, GLA_DKH), heads(v, GLA_DVH)
    if use_rope:
        q, k = _axial_rope(q), _axial_rope(k)
    q = q * (GLA_DKH ** -0.5)
    z = jnp.einsum('bld,rdk->rblk', h, w_a1)
    z = jnp.einsum('rblk,rke->rble', z, w_a2) + b_a[:, None, None, :]
    logg = jax.nn.log_sigmoid(z.astype(F32)) / GLA_GATE_TAU
    logg = logg.reshape(2, B, L, GLA_HEADS, GLA_DKH).transpose(0, 1, 3, 2, 4)
    o_f, s_f = _gla_chunked(q, k, v, logg[0], s0[:, 0])
    flip = lambda a: jnp.flip(a, axis=2)
    o_b, s_b = _gla_chunked(flip(q), flip(k), flip(v), flip(logg[1]), s0[:, 1])
    o = (o_f + flip(o_b)).transpose(0, 2, 1, 3)
    o = o * lax.rsqrt(jnp.mean(o * o, axis=-1, keepdims=True) + EPS)
    o = o * norm_g.astype(F32).reshape(GLA_HEADS, GLA_DVH)
    o = o.reshape(B, L, GLA_DV).astype(h.dtype) * jax.nn.silu(g)
    return o @ w_o, jnp.stack([s_f, s_b], axis=1).astype(h.dtype)


def _na_context(h, w_qkv, w_o):
    B, L, _ = h.shape
    q, k, v = [a.reshape(B, L, NA_HEADS, NA_DH) for a in jnp.split(h @ w_qkv, 3, axis=-1)]
    s = jnp.einsum('bqhd,bkhd->bhqk', q, k).astype(F32) * (NA_DH ** -0.5)
    p = jax.nn.softmax(s, axis=-1).astype(v.dtype)
    o = jnp.einsum('bhqk,bkhd->bqhd', p, v).reshape(B, L, D)
    return o @ w_o, k, v


def _na_latent(h, k_ctx, v_ctx, w_qkv, rpb, w_o):
    B, L, _ = h.shape
    rows = L // GRID_W
    wr = min(NA_WIN_R, rows)
    q, k, v = [a.reshape(B, rows, GRID_W, NA_HEADS, NA_DH) for a in jnp.split(h @ w_qkv, 3, axis=-1)]
    r = jnp.arange(rows)
    r_start = jnp.clip(r - wr // 2, 0, rows - wr)
    row_idx = r_start[:, None] + jnp.arange(wr)
    kw = k[:, row_idx]
    vw = v[:, row_idx].reshape(B, rows, wr * GRID_W, NA_HEADS, NA_DH)
    scale = NA_DH ** -0.5
    s_loc = jnp.einsum('brqhd,brikhd->brhqik', q, kw).astype(F32) * scale
    cidx = jnp.arange(GRID_W)
    c_start = jnp.clip(cidx - NA_WIN_C // 2, 0, GRID_W - NA_WIN_C)
    col_ok = (cidx[None, :] >= c_start[:, None]) & (cidx[None, :] < c_start[:, None] + NA_WIN_C)
    dr = row_idx - r[:, None] + (NA_WIN_R - 1)
    dc = jnp.clip(cidx[None, :] - cidx[:, None] + (NA_WIN_C - 1), 0, 2 * NA_WIN_C - 2)
    bias = rpb[:, dr[:, :, None, None], dc[None, None, :, :]]
    bias = bias.transpose(1, 0, 3, 2, 4).astype(F32)
    s_loc = jnp.where(col_ok[:, None, :], s_loc + bias[None], -jnp.inf)
    s_loc = s_loc.reshape(B, rows, NA_HEADS, GRID_W, wr * GRID_W)
    s_ctx = jnp.einsum('brqhd,bchd->brhqc', q, k_ctx).astype(F32) * scale
    p = jax.nn.softmax(jnp.concatenate([s_loc, s_ctx], axis=-1), axis=-1).astype(v.dtype)
    n_loc = wr * GRID_W
    o = (jnp.einsum('brhqn,brnhd->brqhd', p[..., :n_loc], vw)
         + jnp.einsum('brhqc,bchd->brqhd', p[..., n_loc:], v_ctx))
    return o.reshape(B, L, D) @ w_o


def _final_kernel(x_ref, g_ref, o_ref):
    x = x_ref[...]
    o_ref[...] = x * lax.rsqrt(jnp.mean(x * x, axis=-1, keepdims=True) + EPS) * g_ref[...]


def _final_norm(x, g):
    t = x.shape[0]
    return pl.pallas_call(
        _final_kernel,
        out_shape=jax.ShapeDtypeStruct((t, D), F32),
        grid=(t // TM,),
        in_specs=[pl.BlockSpec((TM, D), lambda i: (i, 0)), pl.BlockSpec((1, D), lambda i: (0, 0))],
        out_specs=pl.BlockSpec((TM, D), lambda i: (i, 0)),
        compiler_params=_cparams("arbitrary"),
        name="final_norm",
    )(x, g)


def kernel(x_prompt, x_sample, state_gla, cache_na_k, cache_na_v, c, c_ctx, norm1_g, norm2_g, w_mod, b_mod, conv_w_in, conv_b_in, conv_dw, conv_dw_b, conv_ln_g, conv_ln_b, conv_w_out, conv_b_out, gla_w_proj, gla_w_a1, gla_w_a2, gla_b_a, gla_norm_g, gla_w_o, na_w_qkv, na_rpb, na_w_o, router_w, moe_w_gate, moe_w_up, moe_w_down, final_g):
    n_p, seq_p, _ = x_prompt.shape
    n_s, seq_s, _ = x_sample.shape
    tpr_p, tpr_s = seq_p // TM, seq_s // TM
    xc = x_prompt.reshape(n_p * seq_p, D)
    xs = x_sample.reshape(n_s * seq_s, D)

    n_c = 1 + n_s
    pad_c = -n_c % 8
    cvec = jnp.concatenate([c_ctx[None, :], c, jnp.zeros((pad_c, D), F32)], axis=0)
    mods = _mod_vectors(cvec, w_mod, b_mod).reshape(DEPTH, n_c + pad_c, 6, D)
    mods = jnp.pad(mods, ((0, 0), (0, 0), (0, 2), (0, 0)))

    new_gla, new_k, new_v = [], [], []
    for i in range(DEPTH):
        kind, j = i % N_MIXERS, i // N_MIXERS
        mod_c = mods[i, 0:1]
        mod_s = mods[i, 1:1 + n_s]
        g1 = norm1_g[i].reshape(1, D)
        g2 = norm2_g[i].reshape(1, D)
        if kind == 0:
            w_in = conv_w_in[j].astype(BF16)
            w_out = conv_w_out[j].astype(BF16)
            vec = lambda a: a.reshape(1, -1)
            args = (conv_dw[j], vec(conv_dw_b[j]), vec(conv_ln_g[j]), vec(conv_ln_b[j]), w_out, vec(conv_b_out[j]))
            u_c = _conv_in(xc, g1, mod_c, w_in, vec(conv_b_in[j]), tpr_p, True)
            u_s = _conv_in(xs, g1, mod_s, w_in, vec(conv_b_in[j]), tpr_s, False)
            xc = _conv_out(xc, u_c, mod_c, *args, tpr_p, True)
            xs = _conv_out(xs, u_s, mod_s, *args, tpr_s, False)
        else:
            mc = [mods[i, 0, m] for m in range(6)]
            mx = [mods[i, 1:1 + n_s, m][:, None, :] for m in range(6)]
            xc3 = xc.reshape(n_p, seq_p, D)
            xs3 = xs.reshape(n_s, seq_s, D)
            hc = _rmsnorm(xc3, norm1_g[i]) * (1 + mc[1]) + mc[0]
            hx = _rmsnorm(xs3, norm1_g[i]) * (1 + mx[1]) + mx[0]
            if kind == 1:
                p = (gla_w_proj[j], gla_w_a1[j], gla_w_a2[j], gla_b_a[j], gla_norm_g[j], gla_w_o[j])
                s_zero = jnp.zeros((n_p, 2, GLA_HEADS, GLA_DKH, GLA_DVH), F32)
                oc, s_ctx = _gla_mixer(hc, s_zero, False, *p)
                ox, _ = _gla_mixer(hx, state_gla[:, j], True, *p)
                new_gla.append(s_ctx)
            else:
                oc, k_c, v_c = _na_context(hc, na_w_qkv[j], na_w_o[j])
                ox = _na_latent(hx, cache_na_k[:, j], cache_na_v[:, j], na_w_qkv[j], na_rpb[j], na_w_o[j])
                new_k.append(k_c)
                new_v.append(v_c)
            xc = (xc3 + mc[2] * oc).reshape(n_p * seq_p, D)
            xs = (xs3 + mx[2] * ox).reshape(n_s * seq_s, D)

        groups = [
            dict(x=xc, mod=mod_c, n_req=n_p, tpr=tpr_p, shared=True, route_batch=n_p, gather_experts=N_EXPERTS),
            dict(x=xs, mod=mod_s, n_req=n_s, tpr=tpr_s, shared=False, route_batch=1, gather_experts=1),
        ]
        xc, xs = _moe(groups, g2, router_w[i].T, moe_w_gate[i], moe_w_up[i], moe_w_down[i])

    fg = final_g.reshape(1, D)
    y_prompt = _final_norm(xc, fg).reshape(n_p, seq_p, D)
    y_sample = _final_norm(xs, fg).reshape(n_s, seq_s, D)
    return (y_prompt, y_sample, jnp.stack(new_gla, axis=1), jnp.stack(new_k, axis=1), jnp.stack(new_v, axis=1))
```

```python
import functools

import jax
import jax.numpy as jnp
import numpy as np
from jax import lax
from jax.experimental import pallas as pl
from jax.experimental.pallas import tpu as pltpu

F32 = jnp.float32
BF16 = jnp.bfloat16
I32 = jnp.int32

D = 1024
DEPTH = 4
GRID_W = 64
N_MIXERS = 3
CONV_WIDTH = 31
CONV_PAD = CONV_WIDTH // 2
GLA_HEADS = 4
GLA_DK = D // 2
GLA_DV = D
GLA_DKH = GLA_DK // GLA_HEADS
GLA_DVH = GLA_DV // GLA_HEADS
GLA_GATE_RANK = 16
GLA_GATE_TAU = 16.0
GLA_CHUNK = 64
NA_HEADS = 16
NA_DH = D // NA_HEADS
NA_WIN_R = 8
NA_WIN_C = 16
N_EXPERTS = 16
EXPERT_CAP_FACTOR = 2
D_EXPERT = 2 * D
ROPE_THETA = 10000.0
EPS = 1e-6

TM = 256
HALO = 16
F_TILE = 512
FFN_ROWS = 512
NA_GROUP = 4
NEG = -1e30
VMEM_LIMIT = 56 * 1024 * 1024
NT = (((1,), (1,)), ((), ()))
TN = (((0,), (0,)), ((), ()))


def _cparams(*sem):
    return pltpu.CompilerParams(dimension_semantics=sem, vmem_limit_bytes=VMEM_LIMIT)


def _norm_mod(x, g, shift, scale):
    ms = jnp.mean(x * x, axis=-1, keepdims=True)
    y = x * lax.rsqrt(ms + EPS) * g
    return y * (1.0 + scale) + shift


def _mod_index(tiles_per_req, shared):
    if shared:
        return lambda i: (0, 0, 0)
    return lambda i: (i // tiles_per_req, 0, 0)


def _mod_kernel(c_ref, w_ref, b_ref, o_ref):
    c = c_ref[...]
    a = (c * jax.nn.sigmoid(c)).astype(BF16)
    o_ref[0] = jnp.dot(a, w_ref[0].astype(BF16), preferred_element_type=F32) + b_ref[0]


def _mod_vectors(cvec, w_mod, b_mod):
    rows = cvec.shape[0]
    n = w_mod.shape[-1]
    tn = D
    return pl.pallas_call(
        _mod_kernel,
        out_shape=jax.ShapeDtypeStruct((DEPTH, rows, n), F32),
        grid=(DEPTH, n // tn),
        in_specs=[
            pl.BlockSpec((rows, D), lambda l, j: (0, 0)),
            pl.BlockSpec((1, D, tn), lambda l, j: (l, 0, j)),
            pl.BlockSpec((1, 1, tn), lambda l, j: (l, 0, j)),
        ],
        out_specs=pl.BlockSpec((1, rows, tn), lambda l, j: (l, 0, j)),
        compiler_params=_cparams("arbitrary", "arbitrary"),
        name="mod_vectors",
    )(cvec, w_mod, b_mod.reshape(DEPTH, 1, n))


def _out_proj_kernel(x_ref, a_ref, mod_ref, w_ref, o_ref):
    res = jnp.dot(a_ref[...], w_ref[...], preferred_element_type=F32)
    o_ref[...] = x_ref[...] + mod_ref[0, 2:3, :] * res


def _out_proj(x, a, mod, w, tiles_per_req, shared):
    t = x.shape[0]
    return pl.pallas_call(
        _out_proj_kernel,
        out_shape=jax.ShapeDtypeStruct((t, D), F32),
        grid=(t // TM,),
        in_specs=[
            pl.BlockSpec((TM, D), lambda i: (i, 0)),
            pl.BlockSpec((TM, D), lambda i: (i, 0)),
            pl.BlockSpec((1, 8, D), _mod_index(tiles_per_req, shared)),
            pl.BlockSpec((D, D), lambda i: (0, 0)),
        ],
        out_specs=pl.BlockSpec((TM, D), lambda i: (i, 0)),
        compiler_params=_cparams("arbitrary"),
        name="out_proj",
    )(x, a, mod, w)


def _conv_in_kernel(x_ref, g_ref, mod_ref, w_ref, b_ref, u_ref):
    h = _norm_mod(x_ref[...], g_ref[...], mod_ref[0, 0:1, :], mod_ref[0, 1:2, :])
    ag = jnp.dot(h.astype(BF16), w_ref[...], preferred_element_type=F32) + b_ref[...]
    u_ref[...] = ag[:, :D] * jax.nn.sigmoid(ag[:, D:])


def _conv_in(x, g, mod, w_in, b_in, tiles_per_req, shared):
    t = x.shape[0]
    return pl.pallas_call(
        _conv_in_kernel,
        out_shape=jax.ShapeDtypeStruct((t, D), F32),
        grid=(t // TM,),
        in_specs=[
            pl.BlockSpec((TM, D), lambda i: (i, 0)),
            pl.BlockSpec((1, D), lambda i: (0, 0)),
            pl.BlockSpec((1, 8, D), _mod_index(tiles_per_req, shared)),
            pl.BlockSpec((D, 2 * D), lambda i: (0, 0)),
            pl.BlockSpec((1, 2 * D), lambda i: (0, 0)),
        ],
        out_specs=pl.BlockSpec((TM, D), lambda i: (i, 0)),
        compiler_params=_cparams("arbitrary"),
        name="conv_in",
    )(x, g, mod, w_in, b_in)


def _conv_out_kernel(tiles_per_req, x_ref, u_ref, up_ref, un_ref, mod_ref, dw_ref, dwb_ref,
                     lng_ref, lnb_ref, w_ref, b_ref, o_ref, win_ref):
    k = pl.program_id(0) % tiles_per_req
    win_ref[0:HALO, :] = jnp.where(k > 0, up_ref[...], 0.0)
    win_ref[HALO:HALO + TM, :] = u_ref[...]
    win_ref[HALO + TM:, :] = jnp.where(k < tiles_per_req - 1, un_ref[...], 0.0)
    rows = 32
    base = HALO - CONV_PAD
    for c in range(TM // rows):
        acc = jnp.zeros((rows, D), F32)
        for tap in range(CONV_WIDTH):
            s = c * rows + base + tap
            acc = acc + dw_ref[tap:tap + 1, :] * win_ref[s:s + rows, :]
        v = acc + dwb_ref[...]
        mu = jnp.mean(v, axis=-1, keepdims=True)
        vc = v - mu
        y = vc * lax.rsqrt(jnp.mean(vc * vc, axis=-1, keepdims=True) + EPS)
        y = y * lng_ref[...] + lnb_ref[...]
        y = y * jax.nn.sigmoid(y)
        res = jnp.dot(y.astype(BF16), w_ref[...], preferred_element_type=F32) + b_ref[...]
        r0 = c * rows
        o_ref[r0:r0 + rows, :] = x_ref[r0:r0 + rows, :] + mod_ref[0, 2:3, :] * res


def _conv_out(x, u, mod, dw, dw_b, ln_g, ln_b, w_out, b_out, tiles_per_req, shared):
    t = x.shape[0]
    hb = TM // HALO
    n_halo = t // HALO
    vec = pl.BlockSpec((1, D), lambda i: (0, 0))
    return pl.pallas_call(
        functools.partial(_conv_out_kernel, tiles_per_req),
        out_shape=jax.ShapeDtypeStruct((t, D), F32),
        grid=(t // TM,),
        in_specs=[
            pl.BlockSpec((TM, D), lambda i: (i, 0)),
            pl.BlockSpec((TM, D), lambda i: (i, 0)),
            pl.BlockSpec((HALO, D), lambda i: (jnp.maximum(i * hb - 1, 0), 0)),
            pl.BlockSpec((HALO, D), lambda i: (jnp.minimum((i + 1) * hb, n_halo - 1), 0)),
            pl.BlockSpec((1, 8, D), _mod_index(tiles_per_req, shared)),
            pl.BlockSpec((CONV_WIDTH, D), lambda i: (0, 0)),
            vec, vec, vec,
            pl.BlockSpec((D, D), lambda i: (0, 0)),
            vec,
        ],
        out_specs=pl.BlockSpec((TM, D), lambda i: (i, 0)),
        scratch_shapes=[pltpu.VMEM((TM + 2 * HALO, D), F32)],
        compiler_params=_cparams("arbitrary"),
        name="conv_out",
    )(x, u, u, u, mod, dw, dw_b, ln_g, ln_b, w_out, b_out)


def _rope_tables(seq):
    half = GLA_DKH // 2
    nf = half // 2
    t = jnp.arange(seq)
    row = (t // GRID_W).astype(F32)
    col = (t % GRID_W).astype(F32)
    inv = ROPE_THETA ** (-jnp.arange(nf, dtype=F32) / nf)
    ar = row[:, None] * inv[None, :]
    ac = col[:, None] * inv[None, :]
    cos = jnp.concatenate([jnp.cos(ar), jnp.cos(ar), jnp.cos(ac), jnp.cos(ac)], axis=1)
    sin = jnp.concatenate([-jnp.sin(ar), jnp.sin(ar), -jnp.sin(ac), jnp.sin(ac)], axis=1)
    return cos, sin


def _gla_proj_kernel(use_rope, *refs):
    if use_rope:
        (x_ref, g_ref, mod_ref, w_ref, wa1_ref, wa2_ref, ba_ref, cos_ref, sin_ref,
         q_ref, k_ref, v_ref, gg_ref, lgf_ref, lgb_ref) = refs
    else:
        (x_ref, g_ref, mod_ref, w_ref, wa1_ref, wa2_ref, ba_ref,
         q_ref, k_ref, v_ref, gg_ref, lgf_ref, lgb_ref) = refs
    h = _norm_mod(x_ref[...], g_ref[...], mod_ref[0, 0:1, :], mod_ref[0, 1:2, :]).astype(BF16)
    p = jnp.dot(h, w_ref[...], preferred_element_type=F32)
    nf = GLA_DKH // 4
    for ref, base, scale in ((q_ref, 0, GLA_DKH ** -0.5), (k_ref, GLA_DK, None)):
        for hd in range(GLA_HEADS):
            a = p[:, base + hd * GLA_DKH: base + (hd + 1) * GLA_DKH]
            if use_rope:
                lane = lax.broadcasted_iota(I32, a.shape, 1)
                partner = jnp.where(lane % (2 * nf) < nf,
                                    pltpu.roll(a, GLA_DKH - nf, 1), pltpu.roll(a, nf, 1))
                a = a * cos_ref[...] + partner * sin_ref[...]
            if scale is not None:
                a = a * scale
            ref[:, hd * GLA_DKH:(hd + 1) * GLA_DKH] = a
    v_ref[...] = p[:, 2 * GLA_DK: 2 * GLA_DK + GLA_DV]
    gg_ref[...] = p[:, 2 * GLA_DK + GLA_DV:]
    z1 = jnp.dot(h, wa1_ref[...], preferred_element_type=F32)
    z = jnp.dot(z1.astype(BF16), wa2_ref[...], preferred_element_type=F32) + ba_ref[...]
    lg = (jnp.minimum(z, 0.0) - jnp.log1p(jnp.exp(-jnp.abs(z)))) / GLA_GATE_TAU
    lgf_ref[...] = lg[:, :GLA_DK]
    lgb_ref[...] = lg[:, GLA_DK:]


def _gla_proj(x, g, mod, w_proj, wa1, wa2, ba, rope, tiles_per_req, shared):
    t = x.shape[0]
    use_rope = rope is not None
    full = lambda a: pl.BlockSpec(a.shape, lambda i: (0,) * a.ndim)
    in_specs = [
        pl.BlockSpec((TM, D), lambda i: (i, 0)),
        pl.BlockSpec((1, D), lambda i: (0, 0)),
        pl.BlockSpec((1, 8, D), _mod_index(tiles_per_req, shared)),
        full(w_proj), full(wa1), full(wa2), full(ba),
    ]
    args = [x, g, mod, w_proj, wa1, wa2, ba]
    if use_rope:
        in_specs += [pl.BlockSpec((TM, GLA_DKH), lambda i: (i % tiles_per_req, 0))] * 2
        args += list(rope)
    row = lambda n: pl.BlockSpec((TM, n), lambda i: (i, 0))
    shp = lambda n: jax.ShapeDtypeStruct((t, n), F32)
    return pl.pallas_call(
        functools.partial(_gla_proj_kernel, use_rope),
        out_shape=(shp(GLA_DK), shp(GLA_DK), shp(GLA_DV), shp(GLA_DV), shp(GLA_DK), shp(GLA_DK)),
        grid=(t // TM,),
        in_specs=in_specs,
        out_specs=(row(GLA_DK), row(GLA_DK), row(GLA_DV), row(GLA_DV), row(GLA_DK), row(GLA_DK)),
        compiler_params=_cparams("arbitrary"),
        name="gla_proj",
    )(*args)


def _gla_core_kernel(n_chunks, has_s0, *refs):
    q_ref, k_ref, v_ref, g_ref, lgf_ref, lgb_ref, ng_ref = refs[:7]
    if has_s0:
        s0_ref, og_ref = refs[7], refs[8]
        sn_ref = None
        scratch = refs[9:]
    else:
        s0_ref = None
        og_ref, sn_ref = refs[7], refs[8]
        scratch = refs[9:]
    oacc_ref, qe_ref, dst_ref, dec_ref = scratch
    C = GLA_CHUNK
    r_i = lax.broadcasted_iota(I32, (C, C), 0)
    c_i = lax.broadcasted_iota(I32, (C, C), 1)

    for d in range(2):
        lg_ref = lgf_ref if d == 0 else lgb_ref
        mask = (c_i <= r_i) if d == 0 else (c_i >= r_i)
        tri = jnp.where(mask, 1.0, 0.0).astype(BF16)
        tot_row = C - 1 if d == 0 else 0

        def chunk(n, carry, lg_ref=lg_ref, mask=mask, tri=tri, tot_row=tot_row, d=d):
            r0 = pl.multiple_of(n * C, C)
            lg = lg_ref[pl.ds(r0, C), :]
            p0 = lg.astype(BF16)
            e1 = lg - p0.astype(F32)
            p1 = e1.astype(BF16)
            p2 = (e1 - p1.astype(F32)).astype(BF16)
            b = (jnp.dot(tri, p0, preferred_element_type=F32)
                 + jnp.dot(tri, p1, preferred_element_type=F32)
                 + jnp.dot(tri, p2, preferred_element_type=F32))
            tot = b[tot_row:tot_row + 1, :]
            q = q_ref[pl.ds(r0, C), :]
            k = k_ref[pl.ds(r0, C), :]
            v = v_ref[pl.ds(r0, C), :].astype(BF16)
            qe = (q * jnp.exp(b)).astype(BF16)
            ke = (k * jnp.exp(-b)).astype(BF16)
            kd = (k * jnp.exp(tot - b)).astype(BF16)
            a = lax.dot_general(qe, ke, NT, preferred_element_type=F32)
            a = jnp.where(mask, a, 0.0).astype(BF16)
            o_in = jnp.dot(a, v, preferred_element_type=F32)
            if d == 0:
                oacc_ref[pl.ds(r0, C), :] = o_in
            else:
                oacc_ref[pl.ds(r0, C), :] = oacc_ref[pl.ds(r0, C), :] + o_in
            qe_ref[pl.ds(r0, C), :] = qe
            dst_ref[n] = lax.dot_general(v, kd, TN, preferred_element_type=F32)
            dec_ref[pl.ds(n, 1), :] = jnp.exp(tot)
            return carry

        lax.fori_loop(0, n_chunks, chunk, 0)

        def scan(i, st, d=d):
            n = i if d == 0 else n_chunks - 1 - i
            r0 = pl.multiple_of(n * C, C)
            o_x = lax.dot_general(qe_ref[pl.ds(r0, C), :], st.astype(BF16), NT, preferred_element_type=F32)
            oacc_ref[pl.ds(r0, C), :] = oacc_ref[pl.ds(r0, C), :] + o_x
            return dec_ref[pl.ds(n, 1), :] * st + dst_ref[n]

        if has_s0:
            st0 = jnp.transpose(s0_ref[0, d, 0])
        else:
            st0 = jnp.zeros((GLA_DVH, GLA_DKH), F32)
        st = lax.fori_loop(0, n_chunks, scan, st0)
        if sn_ref is not None:
            sn_ref[0, d, 0] = jnp.transpose(st)

    rows = 256

    def epilogue(i, carry):
        r0 = pl.multiple_of(i * rows, rows)
        o = oacc_ref[pl.ds(r0, rows), :]
        o = o * lax.rsqrt(jnp.mean(o * o, axis=-1, keepdims=True) + EPS)
        o = o * ng_ref[0]
        g = g_ref[pl.ds(r0, rows), :]
        og_ref[pl.ds(r0, rows), :] = (o * (g * jax.nn.sigmoid(g))).astype(BF16)
        return carry

    lax.fori_loop(0, (n_chunks * C) // rows, epilogue, 0)


def _gla_core(q, k, v, g, lgf, lgb, norm_g, s0, n_req, seq):
    t = q.shape[0]
    n_chunks = seq // GLA_CHUNK
    has_s0 = s0 is not None
    kblk = pl.BlockSpec((seq, GLA_DKH), lambda b, h: (b, h))
    vblk = pl.BlockSpec((seq, GLA_DVH), lambda b, h: (b, h))
    sblk = pl.BlockSpec((1, 2, 1, GLA_DKH, GLA_DVH), lambda b, h: (b, 0, h, 0, 0))
    in_specs = [kblk, kblk, vblk, vblk, kblk, kblk, pl.BlockSpec((1, 1, GLA_DVH), lambda b, h: (h, 0, 0))]
    args = [q, k, v, g, lgf, lgb, norm_g.reshape(GLA_HEADS, 1, GLA_DVH)]
    out_shape = [jax.ShapeDtypeStruct((t, GLA_DV), BF16)]
    out_specs = [vblk]
    if has_s0:
        in_specs.append(sblk)
        args.append(s0)
    else:
        out_shape.append(jax.ShapeDtypeStruct((n_req, 2, GLA_HEADS, GLA_DKH, GLA_DVH), F32))
        out_specs.append(sblk)
    res = pl.pallas_call(
        functools.partial(_gla_core_kernel, n_chunks, has_s0),
        out_shape=tuple(out_shape),
        grid=(n_req, GLA_HEADS),
        in_specs=in_specs,
        out_specs=tuple(out_specs),
        scratch_shapes=[
            pltpu.VMEM((seq, GLA_DVH), F32),
            pltpu.VMEM((seq, GLA_DKH), BF16),
            pltpu.VMEM((n_chunks, GLA_DVH, GLA_DKH), F32),
            pltpu.VMEM((max(n_chunks, 8), GLA_DKH), F32),
        ],
        compiler_params=_cparams("arbitrary", "arbitrary"),
        name="gla_core",
    )(*args)
    return (res[0], None) if has_s0 else res


def _na_qkv_kernel(emit_f32, x_ref, g_ref, mod_ref, w_ref, *outs):
    h = _norm_mod(x_ref[...], g_ref[...], mod_ref[0, 0:1, :], mod_ref[0, 1:2, :]).astype(BF16)
    qkv = jnp.dot(h, w_ref[...], preferred_element_type=F32)
    for n in range(3):
        outs[n][...] = qkv[:, n * D:(n + 1) * D].astype(BF16)
    if emit_f32:
        outs[3][...] = qkv[:, D:2 * D]
        outs[4][...] = qkv[:, 2 * D:]


def _na_qkv(x, g, mod, w_qkv, emit_f32, tiles_per_req, shared):
    t = x.shape[0]
    row = pl.BlockSpec((TM, D), lambda i: (i, 0))
    n_bf, n_f = 3, (2 if emit_f32 else 0)
    return pl.pallas_call(
        functools.partial(_na_qkv_kernel, emit_f32),
        out_shape=tuple([jax.ShapeDtypeStruct((t, D), BF16)] * n_bf + [jax.ShapeDtypeStruct((t, D), F32)] * n_f),
        grid=(t // TM,),
        in_specs=[row, pl.BlockSpec((1, D), lambda i: (0, 0)),
                  pl.BlockSpec((1, 8, D), _mod_index(tiles_per_req, shared)),
                  pl.BlockSpec((D, 3 * D), lambda i: (0, 0))],
        out_specs=tuple([row] * (n_bf + n_f)),
        compiler_params=_cparams("arbitrary"),
        name="na_qkv",
    )(x, g, mod, w_qkv)


def _softmax_rows(parts):
    m = functools.reduce(jnp.maximum, [jnp.max(s, axis=-1, keepdims=True) for s in parts])
    es = [jnp.exp(s - m) for s in parts]
    den = functools.reduce(lambda a, b: a + b, [jnp.sum(e, axis=-1, keepdims=True) for e in es])
    return [(e / den).astype(BF16) for e in es]


def _na_ctx_kernel(q_ref, k_ref, v_ref, o_ref):
    seq = q_ref.shape[0]
    lane = lax.broadcasted_iota(I32, (seq, 2 * NA_DH), 1)
    lo = lane < NA_DH
    scale = NA_DH ** -0.5
    for hp in range(NA_HEADS // 2):
        sl = slice(hp * 2 * NA_DH, (hp + 1) * 2 * NA_DH)
        q, k, v = q_ref[:, sl], k_ref[:, sl], v_ref[:, sl]
        halves = []
        for half in range(2):
            keep = lo if half == 0 else jnp.logical_not(lo)
            qm = jnp.where(keep, q, jnp.zeros_like(q))
            s = lax.dot_general(qm, k, NT, preferred_element_type=F32) * scale
            (p,) = _softmax_rows([s])
            halves.append(jnp.dot(p, v, preferred_element_type=F32))
        o_ref[:, sl] = jnp.where(lo, halves[0], halves[1]).astype(BF16)


def _na_ctx(q, k, v, n_req, seq):
    blk = pl.BlockSpec((seq, D), lambda b: (b, 0))
    return pl.pallas_call(
        _na_ctx_kernel,
        out_shape=jax.ShapeDtypeStruct(q.shape, BF16),
        grid=(n_req,),
        in_specs=[blk, blk, blk],
        out_specs=blk,
        compiler_params=_cparams("arbitrary"),
        name="na_ctx",
    )(q, k, v)


def _na_bias_slabs(rpb, rows):
    wr = min(NA_WIN_R, rows)
    n_g = rows // NA_GROUP
    assert n_g >= 3 and rows % NA_GROUP == 0 and wr // 2 <= NA_GROUP and wr - wr // 2 <= NA_GROUP
    gs = np.array([0, 1, n_g - 1])
    ri = np.arange(NA_GROUP)
    ki = np.arange(3 * NA_GROUP)
    r = gs[:, None] * NA_GROUP + ri[None, :]
    r_start = np.clip(r - wr // 2, 0, rows - wr)
    krow = (gs[:, None] - 1) * NA_GROUP + ki[None, :]
    valid_r = (krow[:, None, :] >= r_start[:, :, None]) & (krow[:, None, :] < r_start[:, :, None] + wr)
    dr = np.clip(krow[:, None, :] - r[:, :, None] + (NA_WIN_R - 1), 0, 2 * NA_WIN_R - 2)
    cidx = np.arange(GRID_W)
    c_start = np.clip(cidx - NA_WIN_C // 2, 0, GRID_W - NA_WIN_C)
    col_ok = (cidx[None, :] >= c_start[:, None]) & (cidx[None, :] < c_start[:, None] + NA_WIN_C)
    dc = np.clip(cidx[None, :] - cidx[:, None] + (NA_WIN_C - 1), 0, 2 * NA_WIN_C - 2)
    bias = rpb[:, dr[:, :, None, :, None], dc[None, None, :, None, :]]
    valid = valid_r[:, :, None, :, None] & col_ok[None, None, :, None, :]
    slab = jnp.where(valid[None], bias.astype(F32), NEG)
    slab = jnp.transpose(slab, (1, 0, 2, 3, 4, 5))
    return slab.reshape(3, NA_HEADS, NA_GROUP * GRID_W, 3 * NA_GROUP * GRID_W)


def _na_lat_kernel(q_ref, kp_ref, kc_ref, kn_ref, vp_ref, vc_ref, vn_ref, kx_ref, vx_ref, slab_ref, o_ref):
    nb, gq, _ = q_ref.shape
    lane = lax.broadcasted_iota(I32, (gq, 2 * NA_DH), 1)
    lo = lane < NA_DH
    scale = NA_DH ** -0.5
    for b in range(nb):
        q = q_ref[b]
        kcat = jnp.concatenate([kp_ref[b], kc_ref[b], kn_ref[b]], axis=0)
        vcat = jnp.concatenate([vp_ref[b], vc_ref[b], vn_ref[b]], axis=0)
        kx = kx_ref[b].astype(BF16)
        vx = vx_ref[b].astype(BF16)
        halves = []
        for half in range(2):
            keep = lo if half == 0 else jnp.logical_not(lo)
            qm = jnp.where(keep, q, jnp.zeros_like(q))
            s_loc = lax.dot_general(qm, kcat, NT, preferred_element_type=F32) * scale + slab_ref[0, half]
            s_ctx = lax.dot_general(qm, kx, NT, preferred_element_type=F32) * scale
            p_loc, p_ctx = _softmax_rows([s_loc, s_ctx])
            halves.append(jnp.dot(p_loc, vcat, preferred_element_type=F32)
                          + jnp.dot(p_ctx, vx, preferred_element_type=F32))
        o_ref[b] = jnp.where(lo, halves[0], halves[1]).astype(BF16)


def _na_lat(q, k, v, k_ctx, v_ctx, slabs, n_req, seq):
    gq = NA_GROUP * GRID_W
    n_g = seq // gq
    past = k_ctx.shape[1]
    q3, k3, v3 = (a.reshape(n_req, seq, D) for a in (q, k, v))
    hp_w = 2 * NA_DH
    cur = pl.BlockSpec((n_req, gq, hp_w), lambda g, hp: (0, g, hp))
    prv = pl.BlockSpec((n_req, gq, hp_w), lambda g, hp: (0, jnp.maximum(g - 1, 0), hp))
    nxt = pl.BlockSpec((n_req, gq, hp_w), lambda g, hp: (0, jnp.minimum(g + 1, n_g - 1), hp))
    ctx = pl.BlockSpec((n_req, past, hp_w), lambda g, hp: (0, 0, hp))
    case = lambda g: jnp.where(g == 0, 0, jnp.where(g == n_g - 1, 2, 1))
    slab = pl.BlockSpec((1, 2, gq, 3 * gq), lambda g, hp: (case(g), hp, 0, 0))
    out = pl.pallas_call(
        _na_lat_kernel,
        out_shape=jax.ShapeDtypeStruct((n_req, seq, D), BF16),
        grid=(n_g, NA_HEADS // 2),
        in_specs=[cur, prv, cur, nxt, prv, cur, nxt, ctx, ctx, slab],
        out_specs=cur,
        compiler_params=_cparams("arbitrary", "arbitrary"),
        name="na_lat",
    )(q3, k3, k3, k3, v3, v3, v3, k_ctx, v_ctx, slabs)
    return out.reshape(n_req * seq, D)


def _split_bf16(a):
    hi = a.astype(BF16)
    lo = (a - hi.astype(F32)).astype(BF16)
    return hi, lo


def _moe_pre_kernel(x_ref, g_ref, mod_ref, rw_ref, h_ref, aff_ref):
    h = _norm_mod(x_ref[...], g_ref[...], mod_ref[0, 3:4, :], mod_ref[0, 4:5, :])
    h_ref[...] = h.astype(BF16)
    h_hi, h_lo = _split_bf16(h)
    r_hi, r_lo = _split_bf16(rw_ref[...])
    logits = (lax.dot_general(r_hi, h_hi, NT, preferred_element_type=F32)
              + lax.dot_general(r_hi, h_lo, NT, preferred_element_type=F32)
              + lax.dot_general(r_lo, h_hi, NT, preferred_element_type=F32))
    m = jnp.max(logits, axis=0, keepdims=True)
    e = jnp.exp(logits - m)
    aff_ref[0] = e / jnp.sum(e, axis=0, keepdims=True)


def _moe_pre(x, g, mod, router_wt, n_req, tiles_per_req, shared):
    t = x.shape[0]
    seq = tiles_per_req * TM
    return pl.pallas_call(
        _moe_pre_kernel,
        out_shape=(jax.ShapeDtypeStruct((t, D), BF16),
                   jax.ShapeDtypeStruct((n_req, N_EXPERTS, seq), F32)),
        grid=(t // TM,),
        in_specs=[
            pl.BlockSpec((TM, D), lambda i: (i, 0)),
            pl.BlockSpec((1, D), lambda i: (0, 0)),
            pl.BlockSpec((1, 8, D), _mod_index(tiles_per_req, shared)),
            pl.BlockSpec((N_EXPERTS, D), lambda i: (0, 0)),
        ],
        out_specs=(pl.BlockSpec((TM, D), lambda i: (i, 0)),
                   pl.BlockSpec((1, N_EXPERTS, TM), lambda i: (i // tiles_per_req, 0, i % tiles_per_req))),
        compiler_params=_cparams("arbitrary"),
        name="moe_pre",
    )(x, g, mod, router_wt)


def _route_kernel(cap, aff_ref, pos_ref):
    aff = aff_ref[...]
    nb, ne, seq = aff.shape
    aff = aff.reshape(nb * ne, seq)
    rows = nb * ne
    bits = pltpu.bitcast(aff, I32)
    thr = jnp.zeros((rows, 1), I32)
    for bit in range(30, -1, -1):
        cand = thr | (1 << bit)
        cnt = jnp.sum(jnp.where(bits >= cand, 1.0, 0.0), axis=1, keepdims=True)
        thr = jnp.where(cnt >= cap, cand, thr)
    gt = bits > thr
    eq = bits == thr
    need = cap - jnp.sum(jnp.where(gt, 1.0, 0.0), axis=1, keepdims=True)
    blk = 256
    r_i = lax.broadcasted_iota(I32, (blk, blk), 0)
    c_i = lax.broadcasted_iota(I32, (blk, blk), 1)
    upper = jnp.where(r_i < c_i, 1.0, 0.0).astype(BF16)
    off_gt = jnp.zeros((rows, 1), F32)
    off_eq = jnp.zeros((rows, 1), F32)
    for c in range(seq // blk):
        sl = slice(c * blk, (c + 1) * blk)
        g = jnp.where(gt[:, sl], 1.0, 0.0)
        q = jnp.where(eq[:, sl], 1.0, 0.0)
        cs_gt = jnp.dot(g.astype(BF16), upper, preferred_element_type=F32) + off_gt
        cs_eq = jnp.dot(q.astype(BF16), upper, preferred_element_type=F32) + off_eq
        off_gt = off_gt + jnp.sum(g, axis=1, keepdims=True)
        off_eq = off_eq + jnp.sum(q, axis=1, keepdims=True)
        sel = (g > 0.5) | ((q > 0.5) & (cs_eq < need))
        pos = cs_gt + jnp.minimum(cs_eq, need)
        pos_ref[:, :, sl] = jnp.where(sel, pos, -1.0).astype(I32).reshape(nb, ne, blk)


def _route(aff, cap, req_per_step):
    n_req, ne, seq = aff.shape
    return pl.pallas_call(
        functools.partial(_route_kernel, cap),
        out_shape=jax.ShapeDtypeStruct((n_req, ne, seq), I32),
        grid=(n_req // req_per_step,),
        in_specs=[pl.BlockSpec((req_per_step, ne, seq), lambda b: (b, 0, 0))],
        out_specs=pl.BlockSpec((req_per_step, ne, seq), lambda b: (b, 0, 0)),
        compiler_params=_cparams("arbitrary"),
        name="moe_route",
    )(aff)


def _gather_kernel(cap, experts_per_step, pos_ref, aff_ref, h_ref, xg_ref, w_ref):
    seq = h_ref.shape[0]
    h = h_ref[...]
    slot = lax.broadcasted_iota(I32, (cap, seq), 0)
    for k in range(experts_per_step):
        e = pl.program_id(1) * experts_per_step + k
        pos = pos_ref[0, pl.ds(e, 1), :]
        aff = aff_ref[0, pl.ds(e, 1), :]
        hit = pos == slot
        onehot = jnp.where(hit, 1.0, 0.0).astype(BF16)
        xg_ref[k] = jnp.dot(onehot, h, preferred_element_type=F32).astype(BF16)
        w_ref[k] = jnp.sum(jnp.where(hit, aff, 0.0), axis=1, keepdims=True)


def _gather(pos, aff, h, cap, experts_per_step):
    n_req, ne, seq = pos.shape
    steps = ne // experts_per_step
    return pl.pallas_call(
        functools.partial(_gather_kernel, cap, experts_per_step),
        out_shape=(jax.ShapeDtypeStruct((ne, n_req * cap, D), BF16),
                   jax.ShapeDtypeStruct((ne, n_req * cap, 1), F32)),
        grid=(n_req, steps),
        in_specs=[
            pl.BlockSpec((1, ne, seq), lambda b, s: (b, 0, 0)),
            pl.BlockSpec((1, ne, seq), lambda b, s: (b, 0, 0)),
            pl.BlockSpec((seq, D), lambda b, s: (b, 0)),
        ],
        out_specs=(pl.BlockSpec((experts_per_step, cap, D), lambda b, s: (s, b, 0)),
                   pl.BlockSpec((experts_per_step, cap, 1), lambda b, s: (s, b, 0))),
        compiler_params=_cparams("arbitrary", "arbitrary"),
        name="moe_gather",
    )(pos, aff, h)


def _ffn_kernel(xp_ref, xs_ref, wp_ref, ws_ref, wg_ref, wu_ref, wd_ref, yp_ref, ys_ref,
                acc_ref, wgb_ref, wub_ref, wdb_ref):
    f = pl.program_id(1)
    last = pl.num_programs(1) - 1
    wgb_ref[...] = wg_ref[0].astype(BF16)
    wub_ref[...] = wu_ref[0].astype(BF16)
    wdb_ref[...] = wd_ref[0].astype(BF16)
    n_p = xp_ref.shape[1]
    n_s = xs_ref.shape[1]
    for x_ref, w_ref, y_ref, base, n in ((xp_ref, wp_ref, yp_ref, 0, n_p), (xs_ref, ws_ref, ys_ref, n_p, n_s)):
        for r0 in range(0, n, FFN_ROWS):
            x = x_ref[0, r0:r0 + FFN_ROWS, :]
            g = jnp.dot(x, wgb_ref[...], preferred_element_type=F32)
            u = jnp.dot(x, wub_ref[...], preferred_element_type=F32)
            hid = (g * jax.nn.sigmoid(g) * u).astype(BF16)
            part = jnp.dot(hid, wdb_ref[...], preferred_element_type=F32)
            rows = slice(base + r0, base + r0 + FFN_ROWS)

            @pl.when(f == 0)
            def _():
                acc_ref[rows, :] = part

            @pl.when(f > 0)
            def _():
                acc_ref[rows, :] = acc_ref[rows, :] + part

            @pl.when(f == last)
            def _():
                y_ref[0, r0:r0 + FFN_ROWS, :] = (acc_ref[rows, :] * w_ref[0, r0:r0 + FFN_ROWS, :]).astype(BF16)


def _ffn(xg_p, xg_s, w_p, w_s, w_gate, w_up, w_down):
    ne, n_p, _ = xg_p.shape
    n_s = xg_s.shape[1]
    fdim = w_gate.shape[-1]
    assert n_p % FFN_ROWS == 0 and n_s % FFN_ROWS == 0 and fdim % F_TILE == 0
    row = lambda n, last: pl.BlockSpec((1, n, last), lambda e, f: (e, 0, 0))
    return pl.pallas_call(
        _ffn_kernel,
        out_shape=(jax.ShapeDtypeStruct((ne, n_p, D), BF16), jax.ShapeDtypeStruct((ne, n_s, D), BF16)),
        grid=(ne, fdim // F_TILE),
        in_specs=[
            row(n_p, D), row(n_s, D), row(n_p, 1), row(n_s, 1),
            pl.BlockSpec((1, D, F_TILE), lambda e, f: (e, 0, f)),
            pl.BlockSpec((1, D, F_TILE), lambda e, f: (e, 0, f)),
            pl.BlockSpec((1, F_TILE, D), lambda e, f: (e, f, 0)),
        ],
        out_specs=(row(n_p, D), row(n_s, D)),
        scratch_shapes=[
            pltpu.VMEM((n_p + n_s, D), F32),
            pltpu.VMEM((D, F_TILE), BF16),
            pltpu.VMEM((D, F_TILE), BF16),
            pltpu.VMEM((F_TILE, D), BF16),
        ],
        compiler_params=_cparams("arbitrary", "arbitrary"),
        name="moe_ffn",
    )(xg_p, xg_s, w_p, w_s, w_gate, w_up, w_down)


def _combine_kernel(cap, x_ref, mod_ref, post_ref, y_ref, o_ref):
    ne = y_ref.shape[0]
    post = post_ref[...]
    if cap % 128 == 0:
        lane = lax.broadcasted_iota(I32, (TM, cap), 1)
        hit = jnp.concatenate(
            [jnp.where(post[:, e:e + 1] == lane, 1.0, 0.0).astype(BF16) for e in range(ne)], axis=1)
    else:
        lane = lax.broadcasted_iota(I32, (TM, ne * cap), 1)
        hit = jnp.zeros((TM, ne * cap), F32)
        for e in range(ne):
            col = post[:, e:e + 1]
            hit = hit + jnp.where((col >= 0) & (col + e * cap == lane), 1.0, 0.0)
        hit = hit.astype(BF16)
    y = y_ref[...].reshape(ne * cap, D)
    res = jnp.dot(hit, y, preferred_element_type=F32)
    o_ref[...] = x_ref[...] + mod_ref[0, 5:6, :] * res


def _combine(x, mod, pos_t, y, cap, tiles_per_req, shared):
    t = x.shape[0]
    ne = y.shape[0]
    return pl.pallas_call(
        functools.partial(_combine_kernel, cap),
        out_shape=jax.ShapeDtypeStruct((t, D), F32),
        grid=(t // TM,),
        in_specs=[
            pl.BlockSpec((TM, D), lambda i: (i, 0)),
            pl.BlockSpec((1, 8, D), _mod_index(tiles_per_req, shared)),
            pl.BlockSpec((TM, ne), lambda i: (i, 0)),
            pl.BlockSpec((ne, cap, D), lambda i: (0, i // tiles_per_req, 0)),
        ],
        out_specs=pl.BlockSpec((TM, D), lambda i: (i, 0)),
        compiler_params=_cparams("arbitrary"),
        name="moe_combine",
    )(x, mod, pos_t, y)


def _moe(groups, g2, router_wt, w_gate, w_up, w_down):
    staged = []
    for gr in groups:
        seq = gr["tpr"] * TM
        cap = EXPERT_CAP_FACTOR * seq // N_EXPERTS
        h, aff = _moe_pre(gr["x"], g2, gr["mod"], router_wt, gr["n_req"], gr["tpr"], gr["shared"])
        pos = _route(aff, cap, gr["route_batch"])
        xg, w = _gather(pos, aff, h, cap, gr["gather_experts"])
        pos_t = jnp.transpose(pos, (0, 2, 1)).reshape(gr["n_req"] * seq, N_EXPERTS)
        staged.append((cap, pos_t, xg, w))
    (_, _, xg_p, w_p), (_, _, xg_s, w_s) = staged
    y_p, y_s = _ffn(xg_p, xg_s, w_p, w_s, w_gate, w_up, w_down)
    outs = []
    for gr, (cap, pos_t, _, _), y in zip(groups, staged, (y_p, y_s)):
        outs.append(_combine(gr["x"], gr["mod"], pos_t, y, cap, gr["tpr"], gr["shared"]))
    return outs


def _final_kernel(x_ref, g_ref, o_ref):
    x = x_ref[...]
    o_ref[...] = x * lax.rsqrt(jnp.mean(x * x, axis=-1, keepdims=True) + EPS) * g_ref[...]


def _final_norm(x, g):
    t = x.shape[0]
    return pl.pallas_call(
        _final_kernel,
        out_shape=jax.ShapeDtypeStruct((t, D), F32),
        grid=(t // TM,),
        in_specs=[pl.BlockSpec((TM, D), lambda i: (i, 0)), pl.BlockSpec((1, D), lambda i: (0, 0))],
        out_specs=pl.BlockSpec((TM, D), lambda i: (i, 0)),
        compiler_params=_cparams("arbitrary"),
        name="final_norm",
    )(x, g)


def kernel(x_prompt, x_sample, state_gla, cache_na_k, cache_na_v, c, c_ctx, norm1_g, norm2_g, w_mod, b_mod, conv_w_in, conv_b_in, conv_dw, conv_dw_b, conv_ln_g, conv_ln_b, conv_w_out, conv_b_out, gla_w_proj, gla_w_a1, gla_w_a2, gla_b_a, gla_norm_g, gla_w_o, na_w_qkv, na_rpb, na_w_o, router_w, moe_w_gate, moe_w_up, moe_w_down, final_g):
    n_p, seq_p, _ = x_prompt.shape
    n_s, seq_s, _ = x_sample.shape
    assert seq_p % TM == 0 and seq_s % TM == 0 and seq_s % GRID_W == 0
    tpr_p, tpr_s = seq_p // TM, seq_s // TM
    xc = x_prompt.reshape(n_p * seq_p, D)
    xs = x_sample.reshape(n_s * seq_s, D)

    n_c = 1 + n_s
    pad_c = -n_c % 8
    cvec = jnp.concatenate([c_ctx[None, :], c, jnp.zeros((pad_c, D), F32)], axis=0)
    mods = _mod_vectors(cvec, w_mod, b_mod).reshape(DEPTH, n_c + pad_c, 6, D)
    mods = jnp.pad(mods, ((0, 0), (0, 0), (0, 2), (0, 0)))
    vec = lambda a: a.reshape(1, -1)

    new_gla, new_k, new_v = [], [], []
    for i in range(DEPTH):
        kind, j = i % N_MIXERS, i // N_MIXERS
        mod_c = mods[i, 0:1]
        mod_s = mods[i, 1:1 + n_s]
        g1 = norm1_g[i].reshape(1, D)
        g2 = norm2_g[i].reshape(1, D)
        if kind == 0:
            w_in = conv_w_in[j].astype(BF16)
            w_out = conv_w_out[j].astype(BF16)
            args = (conv_dw[j], vec(conv_dw_b[j]), vec(conv_ln_g[j]), vec(conv_ln_b[j]), w_out, vec(conv_b_out[j]))
            u_c = _conv_in(xc, g1, mod_c, w_in, vec(conv_b_in[j]), tpr_p, True)
            u_s = _conv_in(xs, g1, mod_s, w_in, vec(conv_b_in[j]), tpr_s, False)
            xc = _conv_out(xc, u_c, mod_c, *args, tpr_p, True)
            xs = _conv_out(xs, u_s, mod_s, *args, tpr_s, False)
        elif kind == 1:
            w_proj = gla_w_proj[j].astype(BF16)
            w_o = gla_w_o[j].astype(BF16)
            wa1 = jnp.concatenate([gla_w_a1[j, 0], gla_w_a1[j, 1]], axis=1).astype(BF16)
            wa2 = jnp.zeros((2 * GLA_GATE_RANK, 2 * GLA_DK), F32)
            wa2 = wa2.at[:GLA_GATE_RANK, :GLA_DK].set(gla_w_a2[j, 0]).at[GLA_GATE_RANK:, GLA_DK:].set(gla_w_a2[j, 1])
            wa2 = wa2.astype(BF16)
            ba = gla_b_a[j].reshape(1, 2 * GLA_DK)
            pc = _gla_proj(xc, g1, mod_c, w_proj, wa1, wa2, ba, None, tpr_p, True)
            ps = _gla_proj(xs, g1, mod_s, w_proj, wa1, wa2, ba, _rope_tables(seq_s), tpr_s, False)
            og_c, s_ctx = _gla_core(*pc, gla_norm_g[j], None, n_p, seq_p)
            og_s, _ = _gla_core(*ps, gla_norm_g[j], state_gla[:, j], n_s, seq_s)
            new_gla.append(s_ctx)
            xc = _out_proj(xc, og_c, mod_c, w_o, tpr_p, True)
            xs = _out_proj(xs, og_s, mod_s, w_o, tpr_s, False)
        else:
            w_qkv = na_w_qkv[j].astype(BF16)
            w_o = na_w_o[j].astype(BF16)
            q_c, k_c, v_c, kf_c, vf_c = _na_qkv(xc, g1, mod_c, w_qkv, True, tpr_p, True)
            q_s, k_s, v_s = _na_qkv(xs, g1, mod_s, w_qkv, False, tpr_s, False)
            o_c = _na_ctx(q_c, k_c, v_c, n_p, seq_p)
            past = cache_na_k.shape[2]
            slabs = _na_bias_slabs(na_rpb[j], seq_s // GRID_W)
            o_s = _na_lat(q_s, k_s, v_s, cache_na_k[:, j].reshape(n_s, past, D),
                          cache_na_v[:, j].reshape(n_s, past, D), slabs, n_s, seq_s)
            new_k.append(kf_c.reshape(n_p, seq_p, NA_HEADS, NA_DH))
            new_v.append(vf_c.reshape(n_p, seq_p, NA_HEADS, NA_DH))
            xc = _out_proj(xc, o_c, mod_c, w_o, tpr_p, True)
            xs = _out_proj(xs, o_s, mod_s, w_o, tpr_s, False)

        groups = [
            dict(x=xc, mod=mod_c, n_req=n_p, tpr=tpr_p, shared=True, route_batch=n_p, gather_experts=N_EXPERTS),
            dict(x=xs, mod=mod_s, n_req=n_s, tpr=tpr_s, shared=False, route_batch=1, gather_experts=1),
        ]
        xc, xs = _moe(groups, g2, router_w[i].T, moe_w_gate[i], moe_w_up[i], moe_w_down[i])

    fg = final_g.reshape(1, D)
    y_prompt = _final_norm(xc, fg).reshape(n_p, seq_p, D)
    y_sample = _final_norm(xs, fg).reshape(n_s, seq_s, D)
    return (y_prompt, y_sample, jnp.stack(new_gla, axis=1), jnp.stack(new_k, axis=1), jnp.stack(new_v, axis=1))
```

```python
import functools

import jax
import jax.numpy as jnp
import numpy as np
from jax import lax
from jax.experimental import pallas as pl
from jax.experimental.pallas import tpu as pltpu

F32 = jnp.float32
BF16 = jnp.bfloat16
I32 = jnp.int32

D = 1024
DEPTH = 4
GRID_W = 64
N_MIXERS = 3
CONV_WIDTH = 31
CONV_PAD = CONV_WIDTH // 2
GLA_HEADS = 4
GLA_DK = D // 2
GLA_DV = D
GLA_DKH = GLA_DK // GLA_HEADS
GLA_DVH = GLA_DV // GLA_HEADS
GLA_GATE_RANK = 16
GLA_GATE_TAU = 16.0
GLA_CHUNK = 64
NA_HEADS = 16
NA_DH = D // NA_HEADS
NA_WIN_R = 8
NA_WIN_C = 16
N_EXPERTS = 16
EXPERT_CAP_FACTOR = 2
D_EXPERT = 2 * D
ROPE_THETA = 10000.0
EPS = 1e-6

TM = 256
HALO = 16
F_TILE = 512
FFN_ROWS = 512
GLA_UNROLL = 4
NA_GROUP = 4
NEG = -1e30
VMEM_LIMIT = 56 * 1024 * 1024
NT = (((1,), (1,)), ((), ()))
TN = (((0,), (0,)), ((), ()))


def _cparams(*sem):
    return pltpu.CompilerParams(dimension_semantics=sem, vmem_limit_bytes=VMEM_LIMIT)


def _norm_mod(x, g, shift, scale):
    ms = jnp.mean(x * x, axis=-1, keepdims=True)
    y = x * lax.rsqrt(ms + EPS) * g
    return y * (1.0 + scale) + shift


def _mod_index(tiles_per_req, shared):
    if shared:
        return lambda i: (0, 0, 0)
    return lambda i: (i // tiles_per_req, 0, 0)


def _mod_kernel(c_ref, w_ref, b_ref, o_ref):
    c = c_ref[...]
    a = (c * jax.nn.sigmoid(c)).astype(BF16)
    o_ref[0] = jnp.dot(a, w_ref[0].astype(BF16), preferred_element_type=F32) + b_ref[0]


def _mod_vectors(cvec, w_mod, b_mod):
    rows = cvec.shape[0]
    n = w_mod.shape[-1]
    tn = D
    return pl.pallas_call(
        _mod_kernel,
        out_shape=jax.ShapeDtypeStruct((DEPTH, rows, n), F32),
        grid=(DEPTH, n // tn),
        in_specs=[
            pl.BlockSpec((rows, D), lambda l, j: (0, 0)),
            pl.BlockSpec((1, D, tn), lambda l, j: (l, 0, j)),
            pl.BlockSpec((1, 1, tn), lambda l, j: (l, 0, j)),
        ],
        out_specs=pl.BlockSpec((1, rows, tn), lambda l, j: (l, 0, j)),
        compiler_params=_cparams("arbitrary", "arbitrary"),
        name="mod_vectors",
    )(cvec, w_mod, b_mod.reshape(DEPTH, 1, n))


def _out_proj_kernel(x_ref, a_ref, mod_ref, w_ref, o_ref):
    res = jnp.dot(a_ref[...], w_ref[...], preferred_element_type=F32)
    o_ref[...] = x_ref[...] + mod_ref[0, 2:3, :] * res


def _out_proj(x, a, mod, w, tiles_per_req, shared):
    t = x.shape[0]
    return pl.pallas_call(
        _out_proj_kernel,
        out_shape=jax.ShapeDtypeStruct((t, D), F32),
        grid=(t // TM,),
        in_specs=[
            pl.BlockSpec((TM, D), lambda i: (i, 0)),
            pl.BlockSpec((TM, D), lambda i: (i, 0)),
            pl.BlockSpec((1, 8, D), _mod_index(tiles_per_req, shared)),
            pl.BlockSpec((D, D), lambda i: (0, 0)),
        ],
        out_specs=pl.BlockSpec((TM, D), lambda i: (i, 0)),
        compiler_params=_cparams("arbitrary"),
        name="out_proj",
    )(x, a, mod, w)


def _conv_in_kernel(x_ref, g_ref, mod_ref, w_ref, b_ref, u_ref):
    h = _norm_mod(x_ref[...], g_ref[...], mod_ref[0, 0:1, :], mod_ref[0, 1:2, :])
    ag = jnp.dot(h.astype(BF16), w_ref[...], preferred_element_type=F32) + b_ref[...]
    u_ref[...] = ag[:, :D] * jax.nn.sigmoid(ag[:, D:])


def _conv_in(x, g, mod, w_in, b_in, tiles_per_req, shared):
    t = x.shape[0]
    return pl.pallas_call(
        _conv_in_kernel,
        out_shape=jax.ShapeDtypeStruct((t, D), F32),
        grid=(t // TM,),
        in_specs=[
            pl.BlockSpec((TM, D), lambda i: (i, 0)),
            pl.BlockSpec((1, D), lambda i: (0, 0)),
            pl.BlockSpec((1, 8, D), _mod_index(tiles_per_req, shared)),
            pl.BlockSpec((D, 2 * D), lambda i: (0, 0)),
            pl.BlockSpec((1, 2 * D), lambda i: (0, 0)),
        ],
        out_specs=pl.BlockSpec((TM, D), lambda i: (i, 0)),
        compiler_params=_cparams("arbitrary"),
        name="conv_in",
    )(x, g, mod, w_in, b_in)


def _conv_out_kernel(tiles_per_req, x_ref, u_ref, up_ref, un_ref, mod_ref, dw_ref, dwb_ref,
                     lng_ref, lnb_ref, w_ref, b_ref, o_ref, win_ref, cv_ref):
    k = pl.program_id(0) % tiles_per_req
    win_ref[0:HALO, :] = jnp.where(k > 0, up_ref[...], 0.0)
    win_ref[HALO:HALO + TM, :] = u_ref[...]
    win_ref[HALO + TM:, :] = jnp.where(k < tiles_per_req - 1, un_ref[...], 0.0)
    rows, lanes, sub = 64, 256, 8
    base = HALO - CONV_PAD
    span = rows + sub * ((base + CONV_WIDTH - 1) // sub)
    for c in range(TM // rows):
        for l0 in range(0, D, lanes):
            acc = jnp.zeros((rows // sub, sub, lanes), F32)
            for phase in range(sub):
                taps = [t for t in range(CONV_WIDTH) if (base + t) % sub == phase]
                if not taps:
                    continue
                xs = win_ref[c * rows + phase:c * rows + phase + span, l0:l0 + lanes]
                for t in taps:
                    a = (base + t) // sub
                    seg = xs[a * sub:a * sub + rows].reshape(rows // sub, sub, lanes)
                    acc = acc + dw_ref[t, :, l0:l0 + lanes][None] * seg
            cv_ref[c * rows:(c + 1) * rows, l0:l0 + lanes] = acc.reshape(rows, lanes)
    v = cv_ref[...] + dwb_ref[...]
    mu = jnp.mean(v, axis=-1, keepdims=True)
    vc = v - mu
    y = vc * lax.rsqrt(jnp.mean(vc * vc, axis=-1, keepdims=True) + EPS)
    y = y * lng_ref[...] + lnb_ref[...]
    y = y * jax.nn.sigmoid(y)
    res = jnp.dot(y.astype(BF16), w_ref[...], preferred_element_type=F32) + b_ref[...]
    o_ref[...] = x_ref[...] + mod_ref[0, 2:3, :] * res


def _conv_out(x, u, mod, dw, dw_b, ln_g, ln_b, w_out, b_out, tiles_per_req, shared):
    t = x.shape[0]
    hb = TM // HALO
    n_halo = t // HALO
    vec = pl.BlockSpec((1, D), lambda i: (0, 0))
    return pl.pallas_call(
        functools.partial(_conv_out_kernel, tiles_per_req),
        out_shape=jax.ShapeDtypeStruct((t, D), F32),
        grid=(t // TM,),
        in_specs=[
            pl.BlockSpec((TM, D), lambda i: (i, 0)),
            pl.BlockSpec((TM, D), lambda i: (i, 0)),
            pl.BlockSpec((HALO, D), lambda i: (jnp.maximum(i * hb - 1, 0), 0)),
            pl.BlockSpec((HALO, D), lambda i: (jnp.minimum((i + 1) * hb, n_halo - 1), 0)),
            pl.BlockSpec((1, 8, D), _mod_index(tiles_per_req, shared)),
            pl.BlockSpec((CONV_WIDTH, 8, D), lambda i: (0, 0, 0)),
            vec, vec, vec,
            pl.BlockSpec((D, D), lambda i: (0, 0)),
            vec,
        ],
        out_specs=pl.BlockSpec((TM, D), lambda i: (i, 0)),
        scratch_shapes=[pltpu.VMEM((TM + 2 * HALO, D), F32), pltpu.VMEM((TM, D), F32)],
        compiler_params=_cparams("arbitrary"),
        name="conv_out",
    )(x, u, u, u, mod, jnp.broadcast_to(dw[:, None, :], (CONV_WIDTH, 8, D)), dw_b, ln_g, ln_b, w_out, b_out)


def _rope_tables(seq):
    half = GLA_DKH // 2
    nf = half // 2
    t = jnp.arange(seq)
    row = (t // GRID_W).astype(F32)
    col = (t % GRID_W).astype(F32)
    inv = ROPE_THETA ** (-jnp.arange(nf, dtype=F32) / nf)
    ar = row[:, None] * inv[None, :]
    ac = col[:, None] * inv[None, :]
    cos = jnp.concatenate([jnp.cos(ar), jnp.cos(ar), jnp.cos(ac), jnp.cos(ac)], axis=1)
    sin = jnp.concatenate([-jnp.sin(ar), jnp.sin(ar), -jnp.sin(ac), jnp.sin(ac)], axis=1)
    return cos, sin


def _gla_proj_kernel(use_rope, *refs):
    if use_rope:
        (x_ref, g_ref, mod_ref, w_ref, wa1_ref, wa2_ref, ba_ref, cos_ref, sin_ref,
         q_ref, k_ref, v_ref, gg_ref, lgf_ref, lgb_ref) = refs
    else:
        (x_ref, g_ref, mod_ref, w_ref, wa1_ref, wa2_ref, ba_ref,
         q_ref, k_ref, v_ref, gg_ref, lgf_ref, lgb_ref) = refs
    h = _norm_mod(x_ref[...], g_ref[...], mod_ref[0, 0:1, :], mod_ref[0, 1:2, :]).astype(BF16)
    p = jnp.dot(h, w_ref[...], preferred_element_type=F32)
    nf = GLA_DKH // 4
    for ref, base, scale in ((q_ref, 0, GLA_DKH ** -0.5), (k_ref, GLA_DK, None)):
        for hd in range(GLA_HEADS):
            a = p[:, base + hd * GLA_DKH: base + (hd + 1) * GLA_DKH]
            if use_rope:
                lane = lax.broadcasted_iota(I32, a.shape, 1)
                partner = jnp.where(lane % (2 * nf) < nf,
                                    pltpu.roll(a, GLA_DKH - nf, 1), pltpu.roll(a, nf, 1))
                a = a * cos_ref[...] + partner * sin_ref[...]
            if scale is not None:
                a = a * scale
            ref[:, hd * GLA_DKH:(hd + 1) * GLA_DKH] = a
    v_ref[...] = p[:, 2 * GLA_DK: 2 * GLA_DK + GLA_DV]
    gg_ref[...] = p[:, 2 * GLA_DK + GLA_DV:]
    z1 = jnp.dot(h, wa1_ref[...], preferred_element_type=F32)
    z = jnp.dot(z1.astype(BF16), wa2_ref[...], preferred_element_type=F32) + ba_ref[...]
    lg = (jnp.minimum(z, 0.0) - jnp.log1p(jnp.exp(-jnp.abs(z)))) / GLA_GATE_TAU
    lgf_ref[...] = lg[:, :GLA_DK]
    lgb_ref[...] = lg[:, GLA_DK:]


def _gla_proj(x, g, mod, w_proj, wa1, wa2, ba, rope, tiles_per_req, shared):
    t = x.shape[0]
    use_rope = rope is not None
    full = lambda a: pl.BlockSpec(a.shape, lambda i: (0,) * a.ndim)
    in_specs = [
        pl.BlockSpec((TM, D), lambda i: (i, 0)),
        pl.BlockSpec((1, D), lambda i: (0, 0)),
        pl.BlockSpec((1, 8, D), _mod_index(tiles_per_req, shared)),
        full(w_proj), full(wa1), full(wa2), full(ba),
    ]
    args = [x, g, mod, w_proj, wa1, wa2, ba]
    if use_rope:
        in_specs += [pl.BlockSpec((TM, GLA_DKH), lambda i: (i % tiles_per_req, 0))] * 2
        args += list(rope)
    row = lambda n: pl.BlockSpec((TM, n), lambda i: (i, 0))
    shp = lambda n: jax.ShapeDtypeStruct((t, n), F32)
    return pl.pallas_call(
        functools.partial(_gla_proj_kernel, use_rope),
        out_shape=(shp(GLA_DK), shp(GLA_DK), shp(GLA_DV), shp(GLA_DV), shp(GLA_DK), shp(GLA_DK)),
        grid=(t // TM,),
        in_specs=in_specs,
        out_specs=(row(GLA_DK), row(GLA_DK), row(GLA_DV), row(GLA_DV), row(GLA_DK), row(GLA_DK)),
        compiler_params=_cparams("arbitrary"),
        name="gla_proj",
    )(*args)


def _gla_core_kernel(n_chunks, has_s0, *refs):
    q_ref, k_ref, v_ref, g_ref, lgf_ref, lgb_ref, ng_ref = refs[:7]
    if has_s0:
        s0_ref, og_ref = refs[7], refs[8]
        sn_ref = None
        scratch = refs[9:]
    else:
        s0_ref = None
        og_ref, sn_ref = refs[7], refs[8]
        scratch = refs[9:]
    oacc_ref, qe_ref, dst_ref, dec_ref = scratch
    C = GLA_CHUNK
    U = GLA_UNROLL
    B = U * C
    r_i = lax.broadcasted_iota(I32, (B, B), 0)
    c_i = lax.broadcasted_iota(I32, (B, B), 1)
    same_chunk = (r_i // C) == (c_i // C)

    for d in range(2):
        lg_ref = lgf_ref if d == 0 else lgb_ref
        mask = same_chunk & ((c_i <= r_i) if d == 0 else (c_i >= r_i))
        tri = jnp.where(mask, 1.0, 0.0).astype(BF16)
        tot_row = C - 1 if d == 0 else 0

        def block(m, carry, lg_ref=lg_ref, mask=mask, tri=tri, tot_row=tot_row, d=d):
            r0 = pl.multiple_of(m * B, B)
            lg = lg_ref[pl.ds(r0, B), :]
            p0 = lg.astype(BF16)
            e1 = lg - p0.astype(F32)
            p1 = e1.astype(BF16)
            p2 = (e1 - p1.astype(F32)).astype(BF16)
            b = (jnp.dot(tri, p0, preferred_element_type=F32)
                 + jnp.dot(tri, p1, preferred_element_type=F32)
                 + jnp.dot(tri, p2, preferred_element_type=F32))
            tots = [b[u * C + tot_row:u * C + tot_row + 1, :] for u in range(U)]
            tot = jnp.concatenate([jnp.broadcast_to(t, (C, GLA_DKH)) for t in tots], axis=0)
            q = q_ref[pl.ds(r0, B), :]
            k = k_ref[pl.ds(r0, B), :]
            v = v_ref[pl.ds(r0, B), :].astype(BF16)
            qe = (q * jnp.exp(b)).astype(BF16)
            ke = (k * jnp.exp(-b)).astype(BF16)
            kd = (k * jnp.exp(tot - b)).astype(BF16)
            a = lax.dot_general(qe, ke, NT, preferred_element_type=F32)
            a = jnp.where(mask, a, 0.0).astype(BF16)
            o_in = jnp.dot(a, v, preferred_element_type=F32)
            if d == 0:
                oacc_ref[pl.ds(r0, B), :] = o_in
            else:
                oacc_ref[pl.ds(r0, B), :] = oacc_ref[pl.ds(r0, B), :] + o_in
            qe_ref[pl.ds(r0, B), :] = qe
            for u in range(U):
                rows = slice(u * C, (u + 1) * C)
                dst_ref[m * U + u] = lax.dot_general(v[rows], kd[rows], TN, preferred_element_type=F32)
                dec_ref[pl.ds(m * U + u, 1), :] = jnp.exp(tots[u])
            return carry

        n_blocks = n_chunks // U
        lax.fori_loop(0, n_blocks, block, 0, unroll=2 if n_blocks % 2 == 0 else 1)

        def scan(i, st, d=d):
            n = i if d == 0 else n_chunks - 1 - i
            r0 = pl.multiple_of(n * C, C)
            o_x = lax.dot_general(qe_ref[pl.ds(r0, C), :], st.astype(BF16), NT, preferred_element_type=F32)
            oacc_ref[pl.ds(r0, C), :] = oacc_ref[pl.ds(r0, C), :] + o_x
            return dec_ref[pl.ds(n, 1), :] * st + dst_ref[n]

        if has_s0:
            st0 = jnp.transpose(s0_ref[0, d, 0])
        else:
            st0 = jnp.zeros((GLA_DVH, GLA_DKH), F32)
        st = lax.fori_loop(0, n_chunks, scan, st0, unroll=GLA_UNROLL)
        if sn_ref is not None:
            sn_ref[0, d, 0] = jnp.transpose(st)

    rows = 256

    def epilogue(i, carry):
        r0 = pl.multiple_of(i * rows, rows)
        o = oacc_ref[pl.ds(r0, rows), :]
        o = o * lax.rsqrt(jnp.mean(o * o, axis=-1, keepdims=True) + EPS)
        o = o * ng_ref[0]
        g = g_ref[pl.ds(r0, rows), :]
        og_ref[pl.ds(r0, rows), :] = (o * (g * jax.nn.sigmoid(g))).astype(BF16)
        return carry

    lax.fori_loop(0, (n_chunks * C) // rows, epilogue, 0)


def _gla_core(q, k, v, g, lgf, lgb, norm_g, s0, n_req, seq):
    t = q.shape[0]
    n_chunks = seq // GLA_CHUNK
    has_s0 = s0 is not None
    kblk = pl.BlockSpec((seq, GLA_DKH), lambda b, h: (b, h))
    vblk = pl.BlockSpec((seq, GLA_DVH), lambda b, h: (b, h))
    sblk = pl.BlockSpec((1, 2, 1, GLA_DKH, GLA_DVH), lambda b, h: (b, 0, h, 0, 0))
    in_specs = [kblk, kblk, vblk, vblk, kblk, kblk, pl.BlockSpec((1, 1, GLA_DVH), lambda b, h: (h, 0, 0))]
    args = [q, k, v, g, lgf, lgb, norm_g.reshape(GLA_HEADS, 1, GLA_DVH)]
    out_shape = [jax.ShapeDtypeStruct((t, GLA_DV), BF16)]
    out_specs = [vblk]
    if has_s0:
        in_specs.append(sblk)
        args.append(s0)
    else:
        out_shape.append(jax.ShapeDtypeStruct((n_req, 2, GLA_HEADS, GLA_DKH, GLA_DVH), F32))
        out_specs.append(sblk)
    res = pl.pallas_call(
        functools.partial(_gla_core_kernel, n_chunks, has_s0),
        out_shape=tuple(out_shape),
        grid=(n_req, GLA_HEADS),
        in_specs=in_specs,
        out_specs=tuple(out_specs),
        scratch_shapes=[
            pltpu.VMEM((seq, GLA_DVH), F32),
            pltpu.VMEM((seq, GLA_DKH), BF16),
            pltpu.VMEM((n_chunks, GLA_DVH, GLA_DKH), F32),
            pltpu.VMEM((max(n_chunks, 8), GLA_DKH), F32),
        ],
        compiler_params=_cparams("arbitrary", "arbitrary"),
        name="gla_core",
    )(*args)
    return (res[0], None) if has_s0 else res


def _na_qkv_kernel(emit_f32, x_ref, g_ref, mod_ref, w_ref, *outs):
    h = _norm_mod(x_ref[...], g_ref[...], mod_ref[0, 0:1, :], mod_ref[0, 1:2, :]).astype(BF16)
    qkv = jnp.dot(h, w_ref[...], preferred_element_type=F32)
    for n in range(3):
        outs[n][...] = qkv[:, n * D:(n + 1) * D].astype(BF16)
    if emit_f32:
        outs[3][...] = qkv[:, D:2 * D]
        outs[4][...] = qkv[:, 2 * D:]


def _na_qkv(x, g, mod, w_qkv, emit_f32, tiles_per_req, shared):
    t = x.shape[0]
    row = pl.BlockSpec((TM, D), lambda i: (i, 0))
    n_bf, n_f = 3, (2 if emit_f32 else 0)
    return pl.pallas_call(
        functools.partial(_na_qkv_kernel, emit_f32),
        out_shape=tuple([jax.ShapeDtypeStruct((t, D), BF16)] * n_bf + [jax.ShapeDtypeStruct((t, D), F32)] * n_f),
        grid=(t // TM,),
        in_specs=[row, pl.BlockSpec((1, D), lambda i: (0, 0)),
                  pl.BlockSpec((1, 8, D), _mod_index(tiles_per_req, shared)),
                  pl.BlockSpec((D, 3 * D), lambda i: (0, 0))],
        out_specs=tuple([row] * (n_bf + n_f)),
        compiler_params=_cparams("arbitrary"),
        name="na_qkv",
    )(x, g, mod, w_qkv)


def _softmax_rows(parts):
    m = functools.reduce(jnp.maximum, [jnp.max(s, axis=-1, keepdims=True) for s in parts])
    es = [jnp.exp(s - m) for s in parts]
    den = functools.reduce(lambda a, b: a + b, [jnp.sum(e, axis=-1, keepdims=True) for e in es])
    return [(e / den).astype(BF16) for e in es]


def _na_ctx_kernel(q_ref, k_ref, v_ref, o_ref):
    seq = q_ref.shape[0]
    lane = lax.broadcasted_iota(I32, (seq, 2 * NA_DH), 1)
    lo = lane < NA_DH
    scale = NA_DH ** -0.5
    for hp in range(NA_HEADS // 2):
        sl = slice(hp * 2 * NA_DH, (hp + 1) * 2 * NA_DH)
        q, k, v = q_ref[:, sl], k_ref[:, sl], v_ref[:, sl]
        halves = []
        for half in range(2):
            keep = lo if half == 0 else jnp.logical_not(lo)
            qm = jnp.where(keep, q, jnp.zeros_like(q))
            s = lax.dot_general(qm, k, NT, preferred_element_type=F32) * scale
            (p,) = _softmax_rows([s])
            halves.append(jnp.dot(p, v, preferred_element_type=F32))
        o_ref[:, sl] = jnp.where(lo, halves[0], halves[1]).astype(BF16)


def _na_ctx(q, k, v, n_req, seq):
    blk = pl.BlockSpec((seq, D), lambda b: (b, 0))
    return pl.pallas_call(
        _na_ctx_kernel,
        out_shape=jax.ShapeDtypeStruct(q.shape, BF16),
        grid=(n_req,),
        in_specs=[blk, blk, blk],
        out_specs=blk,
        compiler_params=_cparams("arbitrary"),
        name="na_ctx",
    )(q, k, v)


def _na_bias_slabs(rpb, rows):
    wr = min(NA_WIN_R, rows)
    n_g = rows // NA_GROUP
    assert n_g >= 3 and rows % NA_GROUP == 0 and wr // 2 <= NA_GROUP and wr - wr // 2 <= NA_GROUP
    gs = np.array([0, 1, n_g - 1])
    ri = np.arange(NA_GROUP)
    ki = np.arange(3 * NA_GROUP)
    r = gs[:, None] * NA_GROUP + ri[None, :]
    r_start = np.clip(r - wr // 2, 0, rows - wr)
    krow = (gs[:, None] - 1) * NA_GROUP + ki[None, :]
    valid_r = (krow[:, None, :] >= r_start[:, :, None]) & (krow[:, None, :] < r_start[:, :, None] + wr)
    dr = np.clip(krow[:, None, :] - r[:, :, None] + (NA_WIN_R - 1), 0, 2 * NA_WIN_R - 2)
    cidx = np.arange(GRID_W)
    c_start = np.clip(cidx - NA_WIN_C // 2, 0, GRID_W - NA_WIN_C)
    col_ok = (cidx[None, :] >= c_start[:, None]) & (cidx[None, :] < c_start[:, None] + NA_WIN_C)
    dc = np.clip(cidx[None, :] - cidx[:, None] + (NA_WIN_C - 1), 0, 2 * NA_WIN_C - 2)
    n_dr, n_dc = rpb.shape[1], rpb.shape[2]
    k_pad = -n_dc % 8
    onehot = (dc.reshape(1, -1) == np.arange(n_dc + k_pad)[:, None]).astype(np.float32)
    rpb2 = jnp.pad(rpb.reshape(NA_HEADS * n_dr, n_dc).astype(F32), ((0, 0), (0, k_pad)))
    table = pl.pallas_call(
        _bias_table_kernel,
        out_shape=jax.ShapeDtypeStruct((NA_HEADS * n_dr, GRID_W * GRID_W), F32),
        name="na_bias_table",
    )(rpb2, jnp.asarray(onehot, BF16))
    table = jnp.where(col_ok, table.reshape(NA_HEADS, n_dr, GRID_W, GRID_W), NEG)
    masked = jnp.full((NA_HEADS, GRID_W, GRID_W), NEG, F32)
    cases = []
    for c in range(3):
        q_rows = []
        for a in range(NA_GROUP):
            blocks = [table[:, int(dr[c, a, b])] if valid_r[c, a, b] else masked for b in range(3 * NA_GROUP)]
            q_rows.append(jnp.concatenate(blocks, axis=2))
        cases.append(jnp.concatenate(q_rows, axis=1))
    return jnp.stack(cases, axis=0)


def _bias_table_kernel(rpb_ref, onehot_ref, t_ref):
    r = rpb_ref[...]
    p0 = r.astype(BF16)
    e1 = r - p0.astype(F32)
    p1 = e1.astype(BF16)
    p2 = (e1 - p1.astype(F32)).astype(BF16)
    oh = onehot_ref[...]
    t_ref[...] = (jnp.dot(p0, oh, preferred_element_type=F32) + jnp.dot(p1, oh, preferred_element_type=F32)
                  + jnp.dot(p2, oh, preferred_element_type=F32))


def _na_lat_kernel(q_ref, kp_ref, kc_ref, kn_ref, vp_ref, vc_ref, vn_ref, kx_ref, vx_ref, slab_ref, o_ref):
    nb, gq, _ = q_ref.shape
    lane = lax.broadcasted_iota(I32, (gq, 2 * NA_DH), 1)
    lo = lane < NA_DH
    scale = NA_DH ** -0.5
    for b in range(nb):
        q = q_ref[b]
        kcat = jnp.concatenate([kp_ref[b], kc_ref[b], kn_ref[b]], axis=0)
        vcat = jnp.concatenate([vp_ref[b], vc_ref[b], vn_ref[b]], axis=0)
        kx = kx_ref[b].astype(BF16)
        vx = vx_ref[b].astype(BF16)
        halves = []
        for half in range(2):
            keep = lo if half == 0 else jnp.logical_not(lo)
            qm = jnp.where(keep, q, jnp.zeros_like(q))
            s_loc = lax.dot_general(qm, kcat, NT, preferred_element_type=F32) * scale + slab_ref[0, half]
            s_ctx = lax.dot_general(qm, kx, NT, preferred_element_type=F32) * scale
            p_loc, p_ctx = _softmax_rows([s_loc, s_ctx])
            halves.append(jnp.dot(p_loc, vcat, preferred_element_type=F32)
                          + jnp.dot(p_ctx, vx, preferred_element_type=F32))
        o_ref[b] = jnp.where(lo, halves[0], halves[1]).astype(BF16)


def _na_lat(q, k, v, k_ctx, v_ctx, slabs, n_req, seq):
    gq = NA_GROUP * GRID_W
    n_g = seq // gq
    past = k_ctx.shape[1]
    q3, k3, v3 = (a.reshape(n_req, seq, D) for a in (q, k, v))
    hp_w = 2 * NA_DH
    cur = pl.BlockSpec((n_req, gq, hp_w), lambda g, hp: (0, g, hp))
    prv = pl.BlockSpec((n_req, gq, hp_w), lambda g, hp: (0, jnp.maximum(g - 1, 0), hp))
    nxt = pl.BlockSpec((n_req, gq, hp_w), lambda g, hp: (0, jnp.minimum(g + 1, n_g - 1), hp))
    ctx = pl.BlockSpec((n_req, past, hp_w), lambda g, hp: (0, 0, hp))
    case = lambda g: jnp.where(g == 0, 0, jnp.where(g == n_g - 1, 2, 1))
    slab = pl.BlockSpec((1, 2, gq, 3 * gq), lambda g, hp: (case(g), hp, 0, 0))
    out = pl.pallas_call(
        _na_lat_kernel,
        out_shape=jax.ShapeDtypeStruct((n_req, seq, D), BF16),
        grid=(n_g, NA_HEADS // 2),
        in_specs=[cur, prv, cur, nxt, prv, cur, nxt, ctx, ctx, slab],
        out_specs=cur,
        compiler_params=_cparams("arbitrary", "arbitrary"),
        name="na_lat",
    )(q3, k3, k3, k3, v3, v3, v3, k_ctx, v_ctx, slabs)
    return out.reshape(n_req * seq, D)


def _split_bf16(a):
    hi = a.astype(BF16)
    lo = (a - hi.astype(F32)).astype(BF16)
    return hi, lo


def _moe_pre_kernel(x_ref, g_ref, mod_ref, rw_ref, h_ref, aff_ref):
    h = _norm_mod(x_ref[...], g_ref[...], mod_ref[0, 3:4, :], mod_ref[0, 4:5, :])
    h_ref[...] = h.astype(BF16)
    h_hi, h_lo = _split_bf16(h)
    r_hi, r_lo = _split_bf16(rw_ref[...])
    logits = (lax.dot_general(r_hi, h_hi, NT, preferred_element_type=F32)
              + lax.dot_general(r_hi, h_lo, NT, preferred_element_type=F32)
              + lax.dot_general(r_lo, h_hi, NT, preferred_element_type=F32))
    m = jnp.max(logits, axis=0, keepdims=True)
    e = jnp.exp(logits - m)
    aff_ref[0] = e / jnp.sum(e, axis=0, keepdims=True)


def _moe_pre(x, g, mod, router_wt, n_req, tiles_per_req, shared):
    t = x.shape[0]
    seq = tiles_per_req * TM
    return pl.pallas_call(
        _moe_pre_kernel,
        out_shape=(jax.ShapeDtypeStruct((t, D), BF16),
                   jax.ShapeDtypeStruct((n_req, N_EXPERTS, seq), F32)),
        grid=(t // TM,),
        in_specs=[
            pl.BlockSpec((TM, D), lambda i: (i, 0)),
            pl.BlockSpec((1, D), lambda i: (0, 0)),
            pl.BlockSpec((1, 8, D), _mod_index(tiles_per_req, shared)),
            pl.BlockSpec((N_EXPERTS, D), lambda i: (0, 0)),
        ],
        out_specs=(pl.BlockSpec((TM, D), lambda i: (i, 0)),
                   pl.BlockSpec((1, N_EXPERTS, TM), lambda i: (i // tiles_per_req, 0, i % tiles_per_req))),
        compiler_params=_cparams("arbitrary"),
        name="moe_pre",
    )(x, g, mod, router_wt)


def _route_kernel(cap, aff_ref, pos_ref):
    aff = aff_ref[...]
    nb, ne, seq = aff.shape
    aff = aff.reshape(nb * ne, seq)
    rows = nb * ne
    bits = pltpu.bitcast(aff, I32)
    thr = jnp.zeros((rows, 1), I32)
    for bit in range(30, -1, -1):
        cand = thr | (1 << bit)
        cnt = jnp.sum(jnp.where(bits >= cand, 1.0, 0.0), axis=1, keepdims=True)
        thr = jnp.where(cnt >= cap, cand, thr)
    gt = bits > thr
    eq = bits == thr
    need = cap - jnp.sum(jnp.where(gt, 1.0, 0.0), axis=1, keepdims=True)
    blk = 256
    r_i = lax.broadcasted_iota(I32, (blk, blk), 0)
    c_i = lax.broadcasted_iota(I32, (blk, blk), 1)
    upper = jnp.where(r_i < c_i, 1.0, 0.0).astype(BF16)
    off_gt = jnp.zeros((rows, 1), F32)
    off_eq = jnp.zeros((rows, 1), F32)
    for c in range(seq // blk):
        sl = slice(c * blk, (c + 1) * blk)
        g = jnp.where(gt[:, sl], 1.0, 0.0)
        q = jnp.where(eq[:, sl], 1.0, 0.0)
        cs_gt = jnp.dot(g.astype(BF16), upper, preferred_element_type=F32) + off_gt
        cs_eq = jnp.dot(q.astype(BF16), upper, preferred_element_type=F32) + off_eq
        off_gt = off_gt + jnp.sum(g, axis=1, keepdims=True)
        off_eq = off_eq + jnp.sum(q, axis=1, keepdims=True)
        sel = (g > 0.5) | ((q > 0.5) & (cs_eq < need))
        pos = cs_gt + jnp.minimum(cs_eq, need)
        pos_ref[:, :, sl] = jnp.where(sel, pos, -1.0).astype(I32).reshape(nb, ne, blk)


def _route(aff, cap, req_per_step):
    n_req, ne, seq = aff.shape
    return pl.pallas_call(
        functools.partial(_route_kernel, cap),
        out_shape=jax.ShapeDtypeStruct((n_req, ne, seq), I32),
        grid=(n_req // req_per_step,),
        in_specs=[pl.BlockSpec((req_per_step, ne, seq), lambda b: (b, 0, 0))],
        out_specs=pl.BlockSpec((req_per_step, ne, seq), lambda b: (b, 0, 0)),
        compiler_params=_cparams("arbitrary"),
        name="moe_route",
    )(aff)


def _gather_kernel(cap, experts_per_step, pos_ref, aff_ref, h_ref, xg_ref, w_ref):
    seq = h_ref.shape[0]
    h = h_ref[...]
    slot = lax.broadcasted_iota(I32, (cap, seq), 0)
    for k in range(experts_per_step):
        e = pl.program_id(1) * experts_per_step + k
        pos = pos_ref[0, pl.ds(e, 1), :]
        aff = aff_ref[0, pl.ds(e, 1), :]
        hit = pos == slot
        onehot = jnp.where(hit, 1.0, 0.0).astype(BF16)
        xg_ref[k] = jnp.dot(onehot, h, preferred_element_type=F32).astype(BF16)
        w_ref[k] = jnp.sum(jnp.where(hit, aff, 0.0), axis=1, keepdims=True)


def _gather(pos, aff, h, cap, experts_per_step):
    n_req, ne, seq = pos.shape
    steps = ne // experts_per_step
    return pl.pallas_call(
        functools.partial(_gather_kernel, cap, experts_per_step),
        out_shape=(jax.ShapeDtypeStruct((ne, n_req * cap, D), BF16),
                   jax.ShapeDtypeStruct((ne, n_req * cap, 1), F32)),
        grid=(n_req, steps),
        in_specs=[
            pl.BlockSpec((1, ne, seq), lambda b, s: (b, 0, 0)),
            pl.BlockSpec((1, ne, seq), lambda b, s: (b, 0, 0)),
            pl.BlockSpec((seq, D), lambda b, s: (b, 0)),
        ],
        out_specs=(pl.BlockSpec((experts_per_step, cap, D), lambda b, s: (s, b, 0)),
                   pl.BlockSpec((experts_per_step, cap, 1), lambda b, s: (s, b, 0))),
        compiler_params=_cparams("arbitrary", "arbitrary"),
        name="moe_gather",
    )(pos, aff, h)


def _ffn_kernel(xp_ref, xs_ref, wp_ref, ws_ref, wg_ref, wu_ref, wd_ref, yp_ref, ys_ref,
                acc_ref, wgb_ref, wub_ref, wdb_ref):
    f = pl.program_id(1)
    last = pl.num_programs(1) - 1
    wgb_ref[...] = wg_ref[0, 0].astype(BF16)
    wub_ref[...] = wu_ref[0, 0].astype(BF16)
    wdb_ref[...] = wd_ref[0, 0].astype(BF16)
    n_p = xp_ref.shape[1]
    n_s = xs_ref.shape[1]
    groups = ((xp_ref, wp_ref, yp_ref, 0, n_p), (xs_ref, ws_ref, ys_ref, n_p, n_s))

    @pl.when(f == 0)
    def _():
        acc_ref[...] = jnp.zeros_like(acc_ref)

    for x_ref, _, _, base, n in groups:
        for r0 in range(0, n, FFN_ROWS):
            x = x_ref[0, r0:r0 + FFN_ROWS, :]
            g = jnp.dot(x, wgb_ref[...], preferred_element_type=F32)
            u = jnp.dot(x, wub_ref[...], preferred_element_type=F32)
            hid = (g * jax.nn.sigmoid(g) * u).astype(BF16)
            rows = slice(base + r0, base + r0 + FFN_ROWS)
            acc_ref[rows, :] = acc_ref[rows, :] + jnp.dot(hid, wdb_ref[...], preferred_element_type=F32)

    @pl.when(f == last)
    def _():
        for _, w_ref, y_ref, base, n in groups:
            y_ref[0] = (acc_ref[base:base + n, :] * w_ref[0]).astype(BF16)


def _ffn(xg_p, xg_s, w_p, w_s, w_gate, w_up, w_down, layer):
    ne, n_p, _ = xg_p.shape
    n_s = xg_s.shape[1]
    fdim = w_gate.shape[-1]
    assert n_p % FFN_ROWS == 0 and n_s % FFN_ROWS == 0 and fdim % F_TILE == 0
    row = lambda n, last: pl.BlockSpec((1, n, last), lambda e, f: (e, 0, 0))
    return pl.pallas_call(
        _ffn_kernel,
        out_shape=(jax.ShapeDtypeStruct((ne, n_p, D), BF16), jax.ShapeDtypeStruct((ne, n_s, D), BF16)),
        grid=(ne, fdim // F_TILE),
        in_specs=[
            row(n_p, D), row(n_s, D), row(n_p, 1), row(n_s, 1),
            pl.BlockSpec((1, 1, D, F_TILE), lambda e, f: (layer, e, 0, f)),
            pl.BlockSpec((1, 1, D, F_TILE), lambda e, f: (layer, e, 0, f)),
            pl.BlockSpec((1, 1, F_TILE, D), lambda e, f: (layer, e, f, 0)),
        ],
        out_specs=(row(n_p, D), row(n_s, D)),
        scratch_shapes=[
            pltpu.VMEM((n_p + n_s, D), F32),
            pltpu.VMEM((D, F_TILE), BF16),
            pltpu.VMEM((D, F_TILE), BF16),
            pltpu.VMEM((F_TILE, D), BF16),
        ],
        compiler_params=_cparams("arbitrary", "arbitrary"),
        name="moe_ffn",
    )(xg_p, xg_s, w_p, w_s, w_gate, w_up, w_down)


def _combine_kernel(cap, x_ref, mod_ref, post_ref, y_ref, o_ref):
    ne = y_ref.shape[0]
    post = post_ref[...]
    if cap % 128 == 0:
        lane = lax.broadcasted_iota(I32, (TM, cap), 1)
        hit = jnp.concatenate(
            [jnp.where(post[:, e:e + 1] == lane, 1.0, 0.0).astype(BF16) for e in range(ne)], axis=1)
    else:
        lane = lax.broadcasted_iota(I32, (TM, ne * cap), 1)
        hit = jnp.zeros((TM, ne * cap), F32)
        for e in range(ne):
            col = post[:, e:e + 1]
            hit = hit + jnp.where((col >= 0) & (col + e * cap == lane), 1.0, 0.0)
        hit = hit.astype(BF16)
    y = y_ref[...].reshape(ne * cap, D)
    res = jnp.dot(hit, y, preferred_element_type=F32)
    o_ref[...] = x_ref[...] + mod_ref[0, 5:6, :] * res


def _combine(x, mod, pos_t, y, cap, tiles_per_req, shared):
    t = x.shape[0]
    ne = y.shape[0]
    return pl.pallas_call(
        functools.partial(_combine_kernel, cap),
        out_shape=jax.ShapeDtypeStruct((t, D), F32),
        grid=(t // TM,),
        in_specs=[
            pl.BlockSpec((TM, D), lambda i: (i, 0)),
            pl.BlockSpec((1, 8, D), _mod_index(tiles_per_req, shared)),
            pl.BlockSpec((TM, ne), lambda i: (i, 0)),
            pl.BlockSpec((ne, cap, D), lambda i: (0, i // tiles_per_req, 0)),
        ],
        out_specs=pl.BlockSpec((TM, D), lambda i: (i, 0)),
        compiler_params=_cparams("arbitrary"),
        name="moe_combine",
    )(x, mod, pos_t, y)


def _moe(groups, g2, router_wt, w_gate, w_up, w_down, layer):
    staged = []
    for gr in groups:
        seq = gr["tpr"] * TM
        cap = EXPERT_CAP_FACTOR * seq // N_EXPERTS
        h, aff = _moe_pre(gr["x"], g2, gr["mod"], router_wt, gr["n_req"], gr["tpr"], gr["shared"])
        pos = _route(aff, cap, gr["route_batch"])
        xg, w = _gather(pos, aff, h, cap, gr["gather_experts"])
        pos_t = jnp.transpose(pos, (0, 2, 1)).reshape(gr["n_req"] * seq, N_EXPERTS)
        staged.append((cap, pos_t, xg, w))
    (_, _, xg_p, w_p), (_, _, xg_s, w_s) = staged
    y_p, y_s = _ffn(xg_p, xg_s, w_p, w_s, w_gate, w_up, w_down, layer)
    outs = []
    for gr, (cap, pos_t, _, _), y in zip(groups, staged, (y_p, y_s)):
        outs.append(_combine(gr["x"], gr["mod"], pos_t, y, cap, gr["tpr"], gr["shared"]))
    return outs


def _final_kernel(x_ref, g_ref, o_ref):
    x = x_ref[...]
    o_ref[...] = x * lax.rsqrt(jnp.mean(x * x, axis=-1, keepdims=True) + EPS) * g_ref[...]


def _final_norm(x, g):
    t = x.shape[0]
    return pl.pallas_call(
        _final_kernel,
        out_shape=jax.ShapeDtypeStruct((t, D), F32),
        grid=(t // TM,),
        in_specs=[pl.BlockSpec((TM, D), lambda i: (i, 0)), pl.BlockSpec((1, D), lambda i: (0, 0))],
        out_specs=pl.BlockSpec((TM, D), lambda i: (i, 0)),
        compiler_params=_cparams("arbitrary"),
        name="final_norm",
    )(x, g)


def kernel(x_prompt, x_sample, state_gla, cache_na_k, cache_na_v, c, c_ctx, norm1_g, norm2_g, w_mod, b_mod, conv_w_in, conv_b_in, conv_dw, conv_dw_b, conv_ln_g, conv_ln_b, conv_w_out, conv_b_out, gla_w_proj, gla_w_a1, gla_w_a2, gla_b_a, gla_norm_g, gla_w_o, na_w_qkv, na_rpb, na_w_o, router_w, moe_w_gate, moe_w_up, moe_w_down, final_g):
    n_p, seq_p, _ = x_prompt.shape
    n_s, seq_s, _ = x_sample.shape
    assert seq_p % TM == 0 and seq_s % TM == 0 and seq_s % GRID_W == 0
    tpr_p, tpr_s = seq_p // TM, seq_s // TM
    xc = x_prompt.reshape(n_p * seq_p, D)
    xs = x_sample.reshape(n_s * seq_s, D)

    n_c = 1 + n_s
    pad_c = -n_c % 8
    cvec = jnp.concatenate([c_ctx[None, :], c, jnp.zeros((pad_c, D), F32)], axis=0)
    mods = _mod_vectors(cvec, w_mod, b_mod).reshape(DEPTH, n_c + pad_c, 6, D)
    mods = jnp.pad(mods, ((0, 0), (0, 0), (0, 2), (0, 0)))
    vec = lambda a: a.reshape(1, -1)

    new_gla, new_k, new_v = [], [], []
    for i in range(DEPTH):
        kind, j = i % N_MIXERS, i // N_MIXERS
        mod_c = mods[i, 0:1]
        mod_s = mods[i, 1:1 + n_s]
        g1 = norm1_g[i].reshape(1, D)
        g2 = norm2_g[i].reshape(1, D)
        if kind == 0:
            w_in = conv_w_in[j].astype(BF16)
            w_out = conv_w_out[j].astype(BF16)
            args = (conv_dw[j], vec(conv_dw_b[j]), vec(conv_ln_g[j]), vec(conv_ln_b[j]), w_out, vec(conv_b_out[j]))
            u_c = _conv_in(xc, g1, mod_c, w_in, vec(conv_b_in[j]), tpr_p, True)
            u_s = _conv_in(xs, g1, mod_s, w_in, vec(conv_b_in[j]), tpr_s, False)
            xc = _conv_out(xc, u_c, mod_c, *args, tpr_p, True)
            xs = _conv_out(xs, u_s, mod_s, *args, tpr_s, False)
        elif kind == 1:
            w_proj = gla_w_proj[j].astype(BF16)
            w_o = gla_w_o[j].astype(BF16)
            wa1 = jnp.concatenate([gla_w_a1[j, 0], gla_w_a1[j, 1]], axis=1).astype(BF16)
            wa2 = jnp.zeros((2 * GLA_GATE_RANK, 2 * GLA_DK), F32)
            wa2 = wa2.at[:GLA_GATE_RANK, :GLA_DK].set(gla_w_a2[j, 0]).at[GLA_GATE_RANK:, GLA_DK:].set(gla_w_a2[j, 1])
            wa2 = wa2.astype(BF16)
            ba = gla_b_a[j].reshape(1, 2 * GLA_DK)
            pc = _gla_proj(xc, g1, mod_c, w_proj, wa1, wa2, ba, None, tpr_p, True)
            ps = _gla_proj(xs, g1, mod_s, w_proj, wa1, wa2, ba, _rope_tables(seq_s), tpr_s, False)
            og_c, s_ctx = _gla_core(*pc, gla_norm_g[j], None, n_p, seq_p)
            og_s, _ = _gla_core(*ps, gla_norm_g[j], state_gla[:, j], n_s, seq_s)
            new_gla.append(s_ctx)
            xc = _out_proj(xc, og_c, mod_c, w_o, tpr_p, True)
            xs = _out_proj(xs, og_s, mod_s, w_o, tpr_s, False)
        else:
            w_qkv = na_w_qkv[j].astype(BF16)
            w_o = na_w_o[j].astype(BF16)
            q_c, k_c, v_c, kf_c, vf_c = _na_qkv(xc, g1, mod_c, w_qkv, True, tpr_p, True)
            q_s, k_s, v_s = _na_qkv(xs, g1, mod_s, w_qkv, False, tpr_s, False)
            o_c = _na_ctx(q_c, k_c, v_c, n_p, seq_p)
            past = cache_na_k.shape[2]
            slabs = _na_bias_slabs(na_rpb[j], seq_s // GRID_W)
            o_s = _na_lat(q_s, k_s, v_s, cache_na_k[:, j].reshape(n_s, past, D),
                          cache_na_v[:, j].reshape(n_s, past, D), slabs, n_s, seq_s)
            new_k.append(kf_c.reshape(n_p, seq_p, NA_HEADS, NA_DH))
            new_v.append(vf_c.reshape(n_p, seq_p, NA_HEADS, NA_DH))
            xc = _out_proj(xc, o_c, mod_c, w_o, tpr_p, True)
            xs = _out_proj(xs, o_s, mod_s, w_o, tpr_s, False)

        groups = [
            dict(x=xc, mod=mod_c, n_req=n_p, tpr=tpr_p, shared=True, route_batch=n_p, gather_experts=N_EXPERTS),
            dict(x=xs, mod=mod_s, n_req=n_s, tpr=tpr_s, shared=False, route_batch=1, gather_experts=1),
        ]
        xc, xs = _moe(groups, g2, router_w[i].T, moe_w_gate, moe_w_up, moe_w_down, i)

    fg = final_g.reshape(1, D)
    y_prompt = _final_norm(xc, fg).reshape(n_p, seq_p, D)
    y_sample = _final_norm(xs, fg).reshape(n_s, seq_s, D)
    return (y_prompt, y_sample, jnp.stack(new_gla, axis=1), jnp.stack(new_k, axis=1), jnp.stack(new_v, axis=1))
```

```python
import functools

import jax
import jax.numpy as jnp
import numpy as np
from jax import lax
from jax.experimental import pallas as pl
from jax.experimental.pallas import tpu as pltpu

F32 = jnp.float32
BF16 = jnp.bfloat16
I32 = jnp.int32

D = 1024
DEPTH = 4
GRID_W = 64
N_MIXERS = 3
CONV_WIDTH = 31
CONV_PAD = CONV_WIDTH // 2
GLA_HEADS = 4
GLA_DK = D // 2
GLA_DV = D
GLA_DKH = GLA_DK // GLA_HEADS
GLA_DVH = GLA_DV // GLA_HEADS
GLA_GATE_RANK = 16
GLA_GATE_TAU = 16.0
GLA_CHUNK = 64
NA_HEADS = 16
NA_DH = D // NA_HEADS
NA_WIN_R = 8
NA_WIN_C = 16
N_EXPERTS = 16
EXPERT_CAP_FACTOR = 2
D_EXPERT = 2 * D
ROPE_THETA = 10000.0
EPS = 1e-6

TM = 256
HALO = 16
F_TILE = 512
FFN_ROWS = 512
ROW_MERGE = 2
COMBINE_TM = 512
GATHER_EXPERTS = 4
GLA_UNROLL = 4
NA_GROUP = 4
NEG = -1e30
VMEM_LIMIT = 56 * 1024 * 1024
NT = (((1,), (1,)), ((), ()))
TN = (((0,), (0,)), ((), ()))


def _cparams(*sem):
    return pltpu.CompilerParams(dimension_semantics=sem, vmem_limit_bytes=VMEM_LIMIT)


def _norm_mod(x, g, shift, scale):
    ms = jnp.mean(x * x, axis=-1, keepdims=True)
    y = x * lax.rsqrt(ms + EPS) * g
    return y * (1.0 + scale) + shift


def _row_tile(tiles_per_req):
    merge = ROW_MERGE if tiles_per_req % ROW_MERGE == 0 else 1
    return TM * merge, tiles_per_req // merge


def _mod_index(tiles_per_req, shared):
    if shared:
        return lambda i: (0, 0, 0)
    return lambda i: (i // tiles_per_req, 0, 0)


def _mod_kernel(c_ref, w_ref, b_ref, o_ref):
    c = c_ref[...]
    a = (c * jax.nn.sigmoid(c)).astype(BF16)
    o_ref[0] = jnp.dot(a, w_ref[0].astype(BF16), preferred_element_type=F32) + b_ref[0]


def _mod_vectors(cvec, w_mod, b_mod):
    rows = cvec.shape[0]
    n = w_mod.shape[-1]
    tn = D
    return pl.pallas_call(
        _mod_kernel,
        out_shape=jax.ShapeDtypeStruct((DEPTH, rows, n), F32),
        grid=(DEPTH, n // tn),
        in_specs=[
            pl.BlockSpec((rows, D), lambda l, j: (0, 0)),
            pl.BlockSpec((1, D, tn), lambda l, j: (l, 0, j)),
            pl.BlockSpec((1, 1, tn), lambda l, j: (l, 0, j)),
        ],
        out_specs=pl.BlockSpec((1, rows, tn), lambda l, j: (l, 0, j)),
        compiler_params=_cparams("arbitrary", "arbitrary"),
        name="mod_vectors",
    )(cvec, w_mod, b_mod.reshape(DEPTH, 1, n))


def _out_proj_kernel(x_ref, a_ref, mod_ref, w_ref, o_ref):
    res = jnp.dot(a_ref[...], w_ref[...], preferred_element_type=F32)
    o_ref[...] = x_ref[...] + mod_ref[0, 2:3, :] * res


def _out_proj(x, a, mod, w, tiles_per_req, shared):
    t = x.shape[0]
    tm, tpr = _row_tile(tiles_per_req)
    return pl.pallas_call(
        _out_proj_kernel,
        out_shape=jax.ShapeDtypeStruct((t, D), F32),
        grid=(t // tm,),
        in_specs=[
            pl.BlockSpec((tm, D), lambda i: (i, 0)),
            pl.BlockSpec((tm, D), lambda i: (i, 0)),
            pl.BlockSpec((1, 8, D), _mod_index(tpr, shared)),
            pl.BlockSpec((D, D), lambda i: (0, 0)),
        ],
        out_specs=pl.BlockSpec((tm, D), lambda i: (i, 0)),
        compiler_params=_cparams("arbitrary"),
        name="out_proj",
    )(x, a, mod, w)


def _conv_in_kernel(x_ref, g_ref, mod_ref, w_ref, b_ref, u_ref):
    h = _norm_mod(x_ref[...], g_ref[...], mod_ref[0, 0:1, :], mod_ref[0, 1:2, :])
    ag = jnp.dot(h.astype(BF16), w_ref[...], preferred_element_type=F32) + b_ref[...]
    u_ref[...] = ag[:, :D] * jax.nn.sigmoid(ag[:, D:])


def _conv_in(x, g, mod, w_in, b_in, tiles_per_req, shared):
    t = x.shape[0]
    tm, tpr = _row_tile(tiles_per_req)
    return pl.pallas_call(
        _conv_in_kernel,
        out_shape=jax.ShapeDtypeStruct((t, D), F32),
        grid=(t // tm,),
        in_specs=[
            pl.BlockSpec((tm, D), lambda i: (i, 0)),
            pl.BlockSpec((1, D), lambda i: (0, 0)),
            pl.BlockSpec((1, 8, D), _mod_index(tpr, shared)),
            pl.BlockSpec((D, 2 * D), lambda i: (0, 0)),
            pl.BlockSpec((1, 2 * D), lambda i: (0, 0)),
        ],
        out_specs=pl.BlockSpec((tm, D), lambda i: (i, 0)),
        compiler_params=_cparams("arbitrary"),
        name="conv_in",
    )(x, g, mod, w_in, b_in)


def _conv_out_kernel(tiles_per_req, x_ref, u_ref, up_ref, un_ref, mod_ref, dw_ref, dwb_ref,
                     lng_ref, lnb_ref, w_ref, b_ref, o_ref, win_ref, cv_ref):
    k = pl.program_id(0) % tiles_per_req
    win_ref[0:HALO, :] = jnp.where(k > 0, up_ref[...], 0.0)
    win_ref[HALO:HALO + TM, :] = u_ref[...]
    win_ref[HALO + TM:, :] = jnp.where(k < tiles_per_req - 1, un_ref[...], 0.0)
    rows, lanes, sub = 64, 256, 8
    base = HALO - CONV_PAD
    span = rows + sub * ((base + CONV_WIDTH - 1) // sub)
    for c in range(TM // rows):
        for l0 in range(0, D, lanes):
            acc = jnp.zeros((rows // sub, sub, lanes), F32)
            for phase in range(sub):
                taps = [t for t in range(CONV_WIDTH) if (base + t) % sub == phase]
                if not taps:
                    continue
                xs = win_ref[c * rows + phase:c * rows + phase + span, l0:l0 + lanes]
                for t in taps:
                    a = (base + t) // sub
                    seg = xs[a * sub:a * sub + rows].reshape(rows // sub, sub, lanes)
                    acc = acc + dw_ref[t, :, l0:l0 + lanes][None] * seg
            cv_ref[c * rows:(c + 1) * rows, l0:l0 + lanes] = acc.reshape(rows, lanes)
    v = cv_ref[...] + dwb_ref[...]
    mu = jnp.mean(v, axis=-1, keepdims=True)
    vc = v - mu
    y = vc * lax.rsqrt(jnp.mean(vc * vc, axis=-1, keepdims=True) + EPS)
    y = y * lng_ref[...] + lnb_ref[...]
    y = y * jax.nn.sigmoid(y)
    res = jnp.dot(y.astype(BF16), w_ref[...], preferred_element_type=F32) + b_ref[...]
    o_ref[...] = x_ref[...] + mod_ref[0, 2:3, :] * res


def _conv_out(x, u, mod, dw, dw_b, ln_g, ln_b, w_out, b_out, tiles_per_req, shared):
    t = x.shape[0]
    hb = TM // HALO
    n_halo = t // HALO
    vec = pl.BlockSpec((1, D), lambda i: (0, 0))
    return pl.pallas_call(
        functools.partial(_conv_out_kernel, tiles_per_req),
        out_shape=jax.ShapeDtypeStruct((t, D), F32),
        grid=(t // TM,),
        in_specs=[
            pl.BlockSpec((TM, D), lambda i: (i, 0)),
            pl.BlockSpec((TM, D), lambda i: (i, 0)),
            pl.BlockSpec((HALO, D), lambda i: (jnp.maximum(i * hb - 1, 0), 0)),
            pl.BlockSpec((HALO, D), lambda i: (jnp.minimum((i + 1) * hb, n_halo - 1), 0)),
            pl.BlockSpec((1, 8, D), _mod_index(tiles_per_req, shared)),
            pl.BlockSpec((CONV_WIDTH, 8, D), lambda i: (0, 0, 0)),
            vec, vec, vec,
            pl.BlockSpec((D, D), lambda i: (0, 0)),
            vec,
        ],
        out_specs=pl.BlockSpec((TM, D), lambda i: (i, 0)),
        scratch_shapes=[pltpu.VMEM((TM + 2 * HALO, D), F32), pltpu.VMEM((TM, D), F32)],
        compiler_params=_cparams("arbitrary"),
        name="conv_out",
    )(x, u, u, u, mod, jnp.broadcast_to(dw[:, None, :], (CONV_WIDTH, 8, D)), dw_b, ln_g, ln_b, w_out, b_out)


def _rope_tables(seq):
    half = GLA_DKH // 2
    nf = half // 2
    t = jnp.arange(seq)
    row = (t // GRID_W).astype(F32)
    col = (t % GRID_W).astype(F32)
    inv = ROPE_THETA ** (-jnp.arange(nf, dtype=F32) / nf)
    ar = row[:, None] * inv[None, :]
    ac = col[:, None] * inv[None, :]
    cos = jnp.concatenate([jnp.cos(ar), jnp.cos(ar), jnp.cos(ac), jnp.cos(ac)], axis=1)
    sin = jnp.concatenate([-jnp.sin(ar), jnp.sin(ar), -jnp.sin(ac), jnp.sin(ac)], axis=1)
    return cos, sin


def _gla_proj_kernel(use_rope, *refs):
    if use_rope:
        (x_ref, g_ref, mod_ref, w_ref, wa1_ref, wa2_ref, ba_ref, cos_ref, sin_ref,
         q_ref, k_ref, v_ref, gg_ref, lgf_ref, lgb_ref) = refs
    else:
        (x_ref, g_ref, mod_ref, w_ref, wa1_ref, wa2_ref, ba_ref,
         q_ref, k_ref, v_ref, gg_ref, lgf_ref, lgb_ref) = refs
    h = _norm_mod(x_ref[...], g_ref[...], mod_ref[0, 0:1, :], mod_ref[0, 1:2, :]).astype(BF16)
    p = jnp.dot(h, w_ref[...], preferred_element_type=F32)
    nf = GLA_DKH // 4
    for ref, base, scale in ((q_ref, 0, GLA_DKH ** -0.5), (k_ref, GLA_DK, None)):
        for hd in range(GLA_HEADS):
            a = p[:, base + hd * GLA_DKH: base + (hd + 1) * GLA_DKH]
            if use_rope:
                lane = lax.broadcasted_iota(I32, a.shape, 1)
                partner = jnp.where(lane % (2 * nf) < nf,
                                    pltpu.roll(a, GLA_DKH - nf, 1), pltpu.roll(a, nf, 1))
                a = a * cos_ref[...] + partner * sin_ref[...]
            if scale is not None:
                a = a * scale
            ref[:, hd * GLA_DKH:(hd + 1) * GLA_DKH] = a
    v_ref[...] = p[:, 2 * GLA_DK: 2 * GLA_DK + GLA_DV]
    gg_ref[...] = p[:, 2 * GLA_DK + GLA_DV:]
    z1 = jnp.dot(h, wa1_ref[...], preferred_element_type=F32)
    z = jnp.dot(z1.astype(BF16), wa2_ref[...], preferred_element_type=F32) + ba_ref[...]
    lg = (jnp.minimum(z, 0.0) - jnp.log1p(jnp.exp(-jnp.abs(z)))) / GLA_GATE_TAU
    lgf_ref[...] = lg[:, :GLA_DK]
    lgb_ref[...] = lg[:, GLA_DK:]


def _gla_proj(x, g, mod, w_proj, wa1, wa2, ba, rope, tiles_per_req, shared):
    t = x.shape[0]
    tm, tpr = _row_tile(tiles_per_req)
    use_rope = rope is not None
    full = lambda a: pl.BlockSpec(a.shape, lambda i: (0,) * a.ndim)
    in_specs = [
        pl.BlockSpec((tm, D), lambda i: (i, 0)),
        pl.BlockSpec((1, D), lambda i: (0, 0)),
        pl.BlockSpec((1, 8, D), _mod_index(tpr, shared)),
        full(w_proj), full(wa1), full(wa2), full(ba),
    ]
    args = [x, g, mod, w_proj, wa1, wa2, ba]
    if use_rope:
        in_specs += [pl.BlockSpec((tm, GLA_DKH), lambda i: (i % tpr, 0))] * 2
        args += list(rope)
    row = lambda n: pl.BlockSpec((tm, n), lambda i: (i, 0))
    shp = lambda n: jax.ShapeDtypeStruct((t, n), F32)
    return pl.pallas_call(
        functools.partial(_gla_proj_kernel, use_rope),
        out_shape=(shp(GLA_DK), shp(GLA_DK), shp(GLA_DV), shp(GLA_DV), shp(GLA_DK), shp(GLA_DK)),
        grid=(t // tm,),
        in_specs=in_specs,
        out_specs=(row(GLA_DK), row(GLA_DK), row(GLA_DV), row(GLA_DV), row(GLA_DK), row(GLA_DK)),
        compiler_params=_cparams("arbitrary"),
        name="gla_proj",
    )(*args)


def _gla_core_kernel(n_chunks, has_s0, *refs):
    q_ref, k_ref, v_ref, g_ref, lgf_ref, lgb_ref, ng_ref = refs[:7]
    if has_s0:
        s0_ref, og_ref = refs[7], refs[8]
        sn_ref = None
        scratch = refs[9:]
    else:
        s0_ref = None
        og_ref, sn_ref = refs[7], refs[8]
        scratch = refs[9:]
    oacc_ref, qe_ref, dst_ref, dec_ref = scratch
    C = GLA_CHUNK
    U = GLA_UNROLL
    B = U * C
    r_i = lax.broadcasted_iota(I32, (B, B), 0)
    c_i = lax.broadcasted_iota(I32, (B, B), 1)
    same_chunk = (r_i // C) == (c_i // C)

    for d in range(2):
        lg_ref = lgf_ref if d == 0 else lgb_ref
        mask = same_chunk & ((c_i <= r_i) if d == 0 else (c_i >= r_i))
        tri = jnp.where(mask, 1.0, 0.0).astype(BF16)
        tot_row = C - 1 if d == 0 else 0

        def block(m, carry, lg_ref=lg_ref, mask=mask, tri=tri, tot_row=tot_row, d=d):
            r0 = pl.multiple_of(m * B, B)
            lg = lg_ref[pl.ds(r0, B), :]
            p0 = lg.astype(BF16)
            e1 = lg - p0.astype(F32)
            p1 = e1.astype(BF16)
            p2 = (e1 - p1.astype(F32)).astype(BF16)
            b = (jnp.dot(tri, p0, preferred_element_type=F32)
                 + jnp.dot(tri, p1, preferred_element_type=F32)
                 + jnp.dot(tri, p2, preferred_element_type=F32))
            tots = [b[u * C + tot_row:u * C + tot_row + 1, :] for u in range(U)]
            tot = jnp.concatenate([jnp.broadcast_to(t, (C, GLA_DKH)) for t in tots], axis=0)
            q = q_ref[pl.ds(r0, B), :]
            k = k_ref[pl.ds(r0, B), :]
            v = v_ref[pl.ds(r0, B), :].astype(BF16)
            qe = (q * jnp.exp(b)).astype(BF16)
            ke = (k * jnp.exp(-b)).astype(BF16)
            kd = (k * jnp.exp(tot - b)).astype(BF16)
            a = lax.dot_general(qe, ke, NT, preferred_element_type=F32)
            a = jnp.where(mask, a, 0.0).astype(BF16)
            o_in = jnp.dot(a, v, preferred_element_type=F32)
            if d == 0:
                oacc_ref[pl.ds(r0, B), :] = o_in
            else:
                oacc_ref[pl.ds(r0, B), :] = oacc_ref[pl.ds(r0, B), :] + o_in
            qe_ref[pl.ds(r0, B), :] = qe
            for u in range(U):
                rows = slice(u * C, (u + 1) * C)
                dst_ref[m * U + u] = lax.dot_general(v[rows], kd[rows], TN, preferred_element_type=F32)
                dec_ref[pl.ds(m * U + u, 1), :] = jnp.exp(tots[u])
            return carry

        n_blocks = n_chunks // U
        lax.fori_loop(0, n_blocks, block, 0, unroll=2 if n_blocks % 2 == 0 else 1)

        def scan(i, st, d=d):
            n = i if d == 0 else n_chunks - 1 - i
            r0 = pl.multiple_of(n * C, C)
            o_x = lax.dot_general(qe_ref[pl.ds(r0, C), :], st.astype(BF16), NT, preferred_element_type=F32)
            oacc_ref[pl.ds(r0, C), :] = oacc_ref[pl.ds(r0, C), :] + o_x
            return dec_ref[pl.ds(n, 1), :] * st + dst_ref[n]

        if has_s0:
            st0 = jnp.transpose(s0_ref[0, d, 0])
        else:
            st0 = jnp.zeros((GLA_DVH, GLA_DKH), F32)
        st = lax.fori_loop(0, n_chunks, scan, st0, unroll=GLA_UNROLL)
        if sn_ref is not None:
            sn_ref[0, d, 0] = jnp.transpose(st)

    rows = 256

    def epilogue(i, carry):
        r0 = pl.multiple_of(i * rows, rows)
        o = oacc_ref[pl.ds(r0, rows), :]
        o = o * lax.rsqrt(jnp.mean(o * o, axis=-1, keepdims=True) + EPS)
        o = o * ng_ref[0]
        g = g_ref[pl.ds(r0, rows), :]
        og_ref[pl.ds(r0, rows), :] = (o * (g * jax.nn.sigmoid(g))).astype(BF16)
        return carry

    lax.fori_loop(0, (n_chunks * C) // rows, epilogue, 0)


def _gla_core(q, k, v, g, lgf, lgb, norm_g, s0, n_req, seq):
    t = q.shape[0]
    n_chunks = seq // GLA_CHUNK
    has_s0 = s0 is not None
    kblk = pl.BlockSpec((seq, GLA_DKH), lambda b, h: (b, h))
    vblk = pl.BlockSpec((seq, GLA_DVH), lambda b, h: (b, h))
    sblk = pl.BlockSpec((1, 2, 1, GLA_DKH, GLA_DVH), lambda b, h: (b, 0, h, 0, 0))
    in_specs = [kblk, kblk, vblk, vblk, kblk, kblk, pl.BlockSpec((1, 1, GLA_DVH), lambda b, h: (h, 0, 0))]
    args = [q, k, v, g, lgf, lgb, norm_g.reshape(GLA_HEADS, 1, GLA_DVH)]
    out_shape = [jax.ShapeDtypeStruct((t, GLA_DV), BF16)]
    out_specs = [vblk]
    if has_s0:
        in_specs.append(sblk)
        args.append(s0)
    else:
        out_shape.append(jax.ShapeDtypeStruct((n_req, 2, GLA_HEADS, GLA_DKH, GLA_DVH), F32))
        out_specs.append(sblk)
    res = pl.pallas_call(
        functools.partial(_gla_core_kernel, n_chunks, has_s0),
        out_shape=tuple(out_shape),
        grid=(n_req, GLA_HEADS),
        in_specs=in_specs,
        out_specs=tuple(out_specs),
        scratch_shapes=[
            pltpu.VMEM((seq, GLA_DVH), F32),
            pltpu.VMEM((seq, GLA_DKH), BF16),
            pltpu.VMEM((n_chunks, GLA_DVH, GLA_DKH), F32),
            pltpu.VMEM((max(n_chunks, 8), GLA_DKH), F32),
        ],
        compiler_params=_cparams("arbitrary", "arbitrary"),
        name="gla_core",
    )(*args)
    return (res[0], None) if has_s0 else res


def _na_qkv_kernel(emit_f32, x_ref, g_ref, mod_ref, w_ref, *outs):
    h = _norm_mod(x_ref[...], g_ref[...], mod_ref[0, 0:1, :], mod_ref[0, 1:2, :]).astype(BF16)
    qkv = jnp.dot(h, w_ref[...], preferred_element_type=F32)
    for n in range(3):
        outs[n][...] = qkv[:, n * D:(n + 1) * D].astype(BF16)
    if emit_f32:
        outs[3][...] = qkv[:, D:2 * D]
        outs[4][...] = qkv[:, 2 * D:]


def _na_qkv(x, g, mod, w_qkv, emit_f32, tiles_per_req, shared):
    t = x.shape[0]
    tm, tpr = _row_tile(tiles_per_req)
    row = pl.BlockSpec((tm, D), lambda i: (i, 0))
    n_bf, n_f = 3, (2 if emit_f32 else 0)
    return pl.pallas_call(
        functools.partial(_na_qkv_kernel, emit_f32),
        out_shape=tuple([jax.ShapeDtypeStruct((t, D), BF16)] * n_bf + [jax.ShapeDtypeStruct((t, D), F32)] * n_f),
        grid=(t // tm,),
        in_specs=[row, pl.BlockSpec((1, D), lambda i: (0, 0)),
                  pl.BlockSpec((1, 8, D), _mod_index(tpr, shared)),
                  pl.BlockSpec((D, 3 * D), lambda i: (0, 0))],
        out_specs=tuple([row] * (n_bf + n_f)),
        compiler_params=_cparams("arbitrary"),
        name="na_qkv",
    )(x, g, mod, w_qkv)


def _softmax_rows(parts):
    m = functools.reduce(jnp.maximum, [jnp.max(s, axis=-1, keepdims=True) for s in parts])
    es = [jnp.exp(s - m) for s in parts]
    den = functools.reduce(lambda a, b: a + b, [jnp.sum(e, axis=-1, keepdims=True) for e in es])
    return [(e / den).astype(BF16) for e in es]


def _na_ctx_kernel(q_ref, k_ref, v_ref, o_ref):
    seq = q_ref.shape[0]
    lane = lax.broadcasted_iota(I32, (seq, 2 * NA_DH), 1)
    lo = lane < NA_DH
    scale = NA_DH ** -0.5
    for hp in range(NA_HEADS // 2):
        sl = slice(hp * 2 * NA_DH, (hp + 1) * 2 * NA_DH)
        q, k, v = q_ref[:, sl], k_ref[:, sl], v_ref[:, sl]
        halves = []
        for half in range(2):
            keep = lo if half == 0 else jnp.logical_not(lo)
            qm = jnp.where(keep, q, jnp.zeros_like(q))
            s = lax.dot_general(qm, k, NT, preferred_element_type=F32) * scale
            (p,) = _softmax_rows([s])
            halves.append(jnp.dot(p, v, preferred_element_type=F32))
        o_ref[:, sl] = jnp.where(lo, halves[0], halves[1]).astype(BF16)


def _na_ctx(q, k, v, n_req, seq):
    blk = pl.BlockSpec((seq, D), lambda b: (b, 0))
    return pl.pallas_call(
        _na_ctx_kernel,
        out_shape=jax.ShapeDtypeStruct(q.shape, BF16),
        grid=(n_req,),
        in_specs=[blk, blk, blk],
        out_specs=blk,
        compiler_params=_cparams("arbitrary"),
        name="na_ctx",
    )(q, k, v)


def _na_bias_slabs(rpb, rows):
    wr = min(NA_WIN_R, rows)
    n_g = rows // NA_GROUP
    assert n_g >= 3 and rows % NA_GROUP == 0 and wr // 2 <= NA_GROUP and wr - wr // 2 <= NA_GROUP
    gs = np.array([0, 1, n_g - 1])
    ri = np.arange(NA_GROUP)
    ki = np.arange(3 * NA_GROUP)
    r = gs[:, None] * NA_GROUP + ri[None, :]
    r_start = np.clip(r - wr // 2, 0, rows - wr)
    krow = (gs[:, None] - 1) * NA_GROUP + ki[None, :]
    valid_r = (krow[:, None, :] >= r_start[:, :, None]) & (krow[:, None, :] < r_start[:, :, None] + wr)
    dr = np.clip(krow[:, None, :] - r[:, :, None] + (NA_WIN_R - 1), 0, 2 * NA_WIN_R - 2)
    cidx = np.arange(GRID_W)
    c_start = np.clip(cidx - NA_WIN_C // 2, 0, GRID_W - NA_WIN_C)
    col_ok = (cidx[None, :] >= c_start[:, None]) & (cidx[None, :] < c_start[:, None] + NA_WIN_C)
    dc = np.clip(cidx[None, :] - cidx[:, None] + (NA_WIN_C - 1), 0, 2 * NA_WIN_C - 2)
    n_dr, n_dc = rpb.shape[1], rpb.shape[2]
    k_pad = -n_dc % 8
    onehot = (dc.reshape(1, -1) == np.arange(n_dc + k_pad)[:, None]).astype(np.float32)
    rpb2 = jnp.pad(rpb.reshape(NA_HEADS * n_dr, n_dc).astype(F32), ((0, 0), (0, k_pad)))
    table = pl.pallas_call(
        _bias_table_kernel,
        out_shape=jax.ShapeDtypeStruct((NA_HEADS * n_dr, GRID_W * GRID_W), F32),
        name="na_bias_table",
    )(rpb2, jnp.asarray(onehot, BF16))
    table = jnp.where(col_ok, table.reshape(NA_HEADS, n_dr, GRID_W, GRID_W), NEG)
    masked = jnp.full((NA_HEADS, GRID_W, GRID_W), NEG, F32)
    cases = []
    for c in range(3):
        q_rows = []
        for a in range(NA_GROUP):
            blocks = [table[:, int(dr[c, a, b])] if valid_r[c, a, b] else masked for b in range(3 * NA_GROUP)]
            q_rows.append(jnp.concatenate(blocks, axis=2))
        cases.append(jnp.concatenate(q_rows, axis=1))
    return jnp.stack(cases, axis=0)


def _bias_table_kernel(rpb_ref, onehot_ref, t_ref):
    r = rpb_ref[...]
    p0 = r.astype(BF16)
    e1 = r - p0.astype(F32)
    p1 = e1.astype(BF16)
    p2 = (e1 - p1.astype(F32)).astype(BF16)
    oh = onehot_ref[...]
    t_ref[...] = (jnp.dot(p0, oh, preferred_element_type=F32) + jnp.dot(p1, oh, preferred_element_type=F32)
                  + jnp.dot(p2, oh, preferred_element_type=F32))


def _na_lat_kernel(q_ref, kp_ref, kc_ref, kn_ref, vp_ref, vc_ref, vn_ref, kx_ref, vx_ref, slab_ref, o_ref):
    nb, gq, _ = q_ref.shape
    lane = lax.broadcasted_iota(I32, (gq, 2 * NA_DH), 1)
    lo = lane < NA_DH
    scale = NA_DH ** -0.5
    assert scale == 2.0 ** round(np.log2(scale))
    slab = slab_ref[0].reshape(2 * gq, slab_ref.shape[-1])
    for b in range(nb):
        q = q_ref[b] * scale
        zero = jnp.zeros_like(q)
        q2 = jnp.concatenate([jnp.where(lo, q, zero), jnp.where(lo, zero, q)], axis=0)
        kcat = jnp.concatenate([kp_ref[b], kc_ref[b], kn_ref[b]], axis=0)
        vcat = jnp.concatenate([vp_ref[b], vc_ref[b], vn_ref[b]], axis=0)
        kx = kx_ref[b].astype(BF16)
        vx = vx_ref[b].astype(BF16)
        s_loc = lax.dot_general(q2, kcat, NT, preferred_element_type=F32) + slab
        s_ctx = lax.dot_general(q2, kx, NT, preferred_element_type=F32)
        m = jnp.maximum(jnp.max(s_loc, axis=-1, keepdims=True), jnp.max(s_ctx, axis=-1, keepdims=True))
        e_loc = jnp.exp(s_loc - m)
        e_ctx = jnp.exp(s_ctx - m)
        den = jnp.sum(e_loc, axis=-1, keepdims=True) + jnp.sum(e_ctx, axis=-1, keepdims=True)
        o2 = (jnp.dot(e_loc.astype(BF16), vcat, preferred_element_type=F32)
              + jnp.dot(e_ctx.astype(BF16), vx, preferred_element_type=F32)) / den
        o_ref[b] = jnp.where(lo, o2[:gq], o2[gq:]).astype(BF16)


def _na_lat(q, k, v, k_ctx, v_ctx, slabs, n_req, seq):
    gq = NA_GROUP * GRID_W
    n_g = seq // gq
    past = k_ctx.shape[1]
    q3, k3, v3 = (a.reshape(n_req, seq, D) for a in (q, k, v))
    hp_w = 2 * NA_DH
    cur = pl.BlockSpec((n_req, gq, hp_w), lambda g, hp: (0, g, hp))
    prv = pl.BlockSpec((n_req, gq, hp_w), lambda g, hp: (0, jnp.maximum(g - 1, 0), hp))
    nxt = pl.BlockSpec((n_req, gq, hp_w), lambda g, hp: (0, jnp.minimum(g + 1, n_g - 1), hp))
    ctx = pl.BlockSpec((n_req, past, hp_w), lambda g, hp: (0, 0, hp))
    case = lambda g: jnp.where(g == 0, 0, jnp.where(g == n_g - 1, 2, 1))
    slab = pl.BlockSpec((1, 2, gq, 3 * gq), lambda g, hp: (case(g), hp, 0, 0))
    out = pl.pallas_call(
        _na_lat_kernel,
        out_shape=jax.ShapeDtypeStruct((n_req, seq, D), BF16),
        grid=(n_g, NA_HEADS // 2),
        in_specs=[cur, prv, cur, nxt, prv, cur, nxt, ctx, ctx, slab],
        out_specs=cur,
        compiler_params=_cparams("arbitrary", "arbitrary"),
        name="na_lat",
    )(q3, k3, k3, k3, v3, v3, v3, k_ctx, v_ctx, slabs)
    return out.reshape(n_req * seq, D)


def _split_bf16(a):
    hi = a.astype(BF16)
    lo = (a - hi.astype(F32)).astype(BF16)
    return hi, lo


def _moe_pre_kernel(x_ref, g_ref, mod_ref, rw_ref, h_ref, aff_ref):
    h = _norm_mod(x_ref[...], g_ref[...], mod_ref[0, 3:4, :], mod_ref[0, 4:5, :])
    h_ref[...] = h.astype(BF16)
    h_hi, h_lo = _split_bf16(h)
    r_hi, r_lo = _split_bf16(rw_ref[...])
    logits = (lax.dot_general(r_hi, h_hi, NT, preferred_element_type=F32)
              + lax.dot_general(r_hi, h_lo, NT, preferred_element_type=F32)
              + lax.dot_general(r_lo, h_hi, NT, preferred_element_type=F32))
    m = jnp.max(logits, axis=0, keepdims=True)
    e = jnp.exp(logits - m)
    aff_ref[0] = e / jnp.sum(e, axis=0, keepdims=True)


def _moe_pre(x, g, mod, router_wt, n_req, tiles_per_req, shared):
    t = x.shape[0]
    seq = tiles_per_req * TM
    tm, tpr = _row_tile(tiles_per_req)
    return pl.pallas_call(
        _moe_pre_kernel,
        out_shape=(jax.ShapeDtypeStruct((t, D), BF16),
                   jax.ShapeDtypeStruct((n_req, N_EXPERTS, seq), F32)),
        grid=(t // tm,),
        in_specs=[
            pl.BlockSpec((tm, D), lambda i: (i, 0)),
            pl.BlockSpec((1, D), lambda i: (0, 0)),
            pl.BlockSpec((1, 8, D), _mod_index(tpr, shared)),
            pl.BlockSpec((N_EXPERTS, D), lambda i: (0, 0)),
        ],
        out_specs=(pl.BlockSpec((tm, D), lambda i: (i, 0)),
                   pl.BlockSpec((1, N_EXPERTS, tm), lambda i: (i // tpr, 0, i % tpr))),
        compiler_params=_cparams("arbitrary"),
        name="moe_pre",
    )(x, g, mod, router_wt)


def _route_kernel(cap, aff_ref, pos_ref):
    aff = aff_ref[...]
    nb, ne, seq = aff.shape
    aff = aff.reshape(nb * ne, seq)
    rows = nb * ne
    bits = pltpu.bitcast(aff, I32)
    thr = jnp.zeros((rows, 1), I32)
    for bit in range(30, -1, -1):
        cand = thr | (1 << bit)
        cnt = jnp.sum(jnp.where(bits >= cand, 1.0, 0.0), axis=1, keepdims=True)
        thr = jnp.where(cnt >= cap, cand, thr)
    gt = bits > thr
    eq = bits == thr
    need = cap - jnp.sum(jnp.where(gt, 1.0, 0.0), axis=1, keepdims=True)
    blk = 256
    r_i = lax.broadcasted_iota(I32, (blk, blk), 0)
    c_i = lax.broadcasted_iota(I32, (blk, blk), 1)
    upper = jnp.where(r_i < c_i, 1.0, 0.0).astype(BF16)
    off_gt = jnp.zeros((rows, 1), F32)
    off_eq = jnp.zeros((rows, 1), F32)
    for c in range(seq // blk):
        sl = slice(c * blk, (c + 1) * blk)
        g = jnp.where(gt[:, sl], 1.0, 0.0)
        q = jnp.where(eq[:, sl], 1.0, 0.0)
        cs_gt = jnp.dot(g.astype(BF16), upper, preferred_element_type=F32) + off_gt
        cs_eq = jnp.dot(q.astype(BF16), upper, preferred_element_type=F32) + off_eq
        off_gt = off_gt + jnp.sum(g, axis=1, keepdims=True)
        off_eq = off_eq + jnp.sum(q, axis=1, keepdims=True)
        sel = (g > 0.5) | ((q > 0.5) & (cs_eq < need))
        pos = cs_gt + jnp.minimum(cs_eq, need)
        pos_ref[:, :, sl] = jnp.where(sel, pos, -1.0).astype(I32).reshape(nb, ne, blk)


def _route(aff, cap, req_per_step):
    n_req, ne, seq = aff.shape
    return pl.pallas_call(
        functools.partial(_route_kernel, cap),
        out_shape=jax.ShapeDtypeStruct((n_req, ne, seq), I32),
        grid=(n_req // req_per_step,),
        in_specs=[pl.BlockSpec((req_per_step, ne, seq), lambda b: (b, 0, 0))],
        out_specs=pl.BlockSpec((req_per_step, ne, seq), lambda b: (b, 0, 0)),
        compiler_params=_cparams("arbitrary"),
        name="moe_route",
    )(aff)


def _gather_kernel(cap, experts_per_step, pos_ref, aff_ref, h_ref, xg_ref, w_ref):
    seq = h_ref.shape[0]
    h = h_ref[...]
    slot = lax.broadcasted_iota(I32, (cap, seq), 0)
    onehots = []
    for k in range(experts_per_step):
        e = pl.program_id(1) * experts_per_step + k
        pos = pos_ref[0, pl.ds(e, 1), :]
        aff = aff_ref[0, pl.ds(e, 1), :]
        hit = pos == slot
        onehots.append(jnp.where(hit, 1.0, 0.0).astype(BF16))
        w_ref[k] = jnp.sum(jnp.where(hit, aff, 0.0), axis=1, keepdims=True)
    xg = jnp.dot(jnp.concatenate(onehots, axis=0), h, preferred_element_type=F32).astype(BF16)
    xg_ref[...] = xg.reshape(experts_per_step, cap, D)


def _gather(pos, aff, h, cap, experts_per_step):
    n_req, ne, seq = pos.shape
    steps = ne // experts_per_step
    return pl.pallas_call(
        functools.partial(_gather_kernel, cap, experts_per_step),
        out_shape=(jax.ShapeDtypeStruct((ne, n_req * cap, D), BF16),
                   jax.ShapeDtypeStruct((ne, n_req * cap, 1), F32)),
        grid=(n_req, steps),
        in_specs=[
            pl.BlockSpec((1, ne, seq), lambda b, s: (b, 0, 0)),
            pl.BlockSpec((1, ne, seq), lambda b, s: (b, 0, 0)),
            pl.BlockSpec((seq, D), lambda b, s: (b, 0)),
        ],
        out_specs=(pl.BlockSpec((experts_per_step, cap, D), lambda b, s: (s, b, 0)),
                   pl.BlockSpec((experts_per_step, cap, 1), lambda b, s: (s, b, 0))),
        compiler_params=_cparams("arbitrary", "arbitrary"),
        name="moe_gather",
    )(pos, aff, h)


def _ffn_kernel(xp_ref, xs_ref, wp_ref, ws_ref, wg_ref, wu_ref, wd_ref, yp_ref, ys_ref,
                acc_ref, wgb_ref, wub_ref, wdb_ref):
    f = pl.program_id(1)
    last = pl.num_programs(1) - 1
    wgb_ref[...] = wg_ref[0, 0].astype(BF16)
    wub_ref[...] = wu_ref[0, 0].astype(BF16)
    wdb_ref[...] = wd_ref[0, 0].astype(BF16)
    n_p = xp_ref.shape[1]
    n_s = xs_ref.shape[1]
    groups = ((xp_ref, wp_ref, yp_ref, 0, n_p), (xs_ref, ws_ref, ys_ref, n_p, n_s))

    @pl.when(f == 0)
    def _():
        acc_ref[...] = jnp.zeros_like(acc_ref)

    for x_ref, _, _, base, n in groups:
        for r0 in range(0, n, FFN_ROWS):
            x = x_ref[0, r0:r0 + FFN_ROWS, :]
            g = jnp.dot(x, wgb_ref[...], preferred_element_type=F32)
            u = jnp.dot(x, wub_ref[...], preferred_element_type=F32)
            hid = (g * jax.nn.sigmoid(g) * u).astype(BF16)
            rows = slice(base + r0, base + r0 + FFN_ROWS)
            acc_ref[rows, :] = acc_ref[rows, :] + jnp.dot(hid, wdb_ref[...], preferred_element_type=F32)

    @pl.when(f == last)
    def _():
        for _, w_ref, y_ref, base, n in groups:
            y_ref[0] = (acc_ref[base:base + n, :] * w_ref[0]).astype(BF16)


def _ffn(xg_p, xg_s, w_p, w_s, w_gate, w_up, w_down, layer):
    ne, n_p, _ = xg_p.shape
    n_s = xg_s.shape[1]
    fdim = w_gate.shape[-1]
    assert n_p % FFN_ROWS == 0 and n_s % FFN_ROWS == 0 and fdim % F_TILE == 0
    row = lambda n, last: pl.BlockSpec((1, n, last), lambda e, f: (e, 0, 0))
    return pl.pallas_call(
        _ffn_kernel,
        out_shape=(jax.ShapeDtypeStruct((ne, n_p, D), BF16), jax.ShapeDtypeStruct((ne, n_s, D), BF16)),
        grid=(ne, fdim // F_TILE),
        in_specs=[
            row(n_p, D), row(n_s, D), row(n_p, 1), row(n_s, 1),
            pl.BlockSpec((1, 1, D, F_TILE), lambda e, f: (layer, e, 0, f)),
            pl.BlockSpec((1, 1, D, F_TILE), lambda e, f: (layer, e, 0, f)),
            pl.BlockSpec((1, 1, F_TILE, D), lambda e, f: (layer, e, f, 0)),
        ],
        out_specs=(row(n_p, D), row(n_s, D)),
        scratch_shapes=[
            pltpu.VMEM((n_p + n_s, D), F32),
            pltpu.VMEM((D, F_TILE), BF16),
            pltpu.VMEM((D, F_TILE), BF16),
            pltpu.VMEM((F_TILE, D), BF16),
        ],
        compiler_params=_cparams("arbitrary", "arbitrary"),
        name="moe_ffn",
    )(xg_p, xg_s, w_p, w_s, w_gate, w_up, w_down)


def _combine_kernel(cap, x_ref, mod_ref, post_ref, y_ref, *rest):
    fg_ref, o_ref = rest if len(rest) == 2 else (None, rest[0])
    ne = y_ref.shape[0]
    tm = x_ref.shape[0]
    post = post_ref[...]
    if cap % 128 == 0:
        lane = lax.broadcasted_iota(I32, (tm, cap), 1)
        hit = jnp.concatenate(
            [jnp.where(post[:, e:e + 1] == lane, 1.0, 0.0).astype(BF16) for e in range(ne)], axis=1)
    else:
        lane = lax.broadcasted_iota(I32, (tm, ne * cap), 1)
        hit = jnp.zeros((tm, ne * cap), F32)
        for e in range(ne):
            col = post[:, e:e + 1]
            hit = hit + jnp.where((col >= 0) & (col + e * cap == lane), 1.0, 0.0)
        hit = hit.astype(BF16)
    y = y_ref[...].reshape(ne * cap, D)
    res = jnp.dot(hit, y, preferred_element_type=F32)
    out = x_ref[...] + mod_ref[0, 5:6, :] * res
    if fg_ref is not None:
        out = out * lax.rsqrt(jnp.mean(out * out, axis=-1, keepdims=True) + EPS) * fg_ref[...]
    o_ref[...] = out


def _combine(x, mod, pos_t, y, cap, seq, shared, final_g):
    t = x.shape[0]
    ne = y.shape[0]
    tm = min(seq, COMBINE_TM)
    tiles_per_req = seq // tm
    in_specs = [
        pl.BlockSpec((tm, D), lambda i: (i, 0)),
        pl.BlockSpec((1, 8, D), _mod_index(tiles_per_req, shared)),
        pl.BlockSpec((tm, ne), lambda i: (i, 0)),
        pl.BlockSpec((ne, cap, D), lambda i: (0, i // tiles_per_req, 0)),
    ]
    args = [x, mod, pos_t, y]
    if final_g is not None:
        in_specs.append(pl.BlockSpec((1, D), lambda i: (0, 0)))
        args.append(final_g)
    return pl.pallas_call(
        functools.partial(_combine_kernel, cap),
        out_shape=jax.ShapeDtypeStruct((t, D), F32),
        grid=(t // tm,),
        in_specs=in_specs,
        out_specs=pl.BlockSpec((tm, D), lambda i: (i, 0)),
        compiler_params=_cparams("arbitrary"),
        name="moe_combine",
    )(*args)


def _moe(groups, g2, router_wt, w_gate, w_up, w_down, layer, final_g=None):
    staged = []
    for gr in groups:
        seq = gr["tpr"] * TM
        cap = EXPERT_CAP_FACTOR * seq // N_EXPERTS
        h, aff = _moe_pre(gr["x"], g2, gr["mod"], router_wt, gr["n_req"], gr["tpr"], gr["shared"])
        pos = _route(aff, cap, gr["route_batch"])
        xg, w = _gather(pos, aff, h, cap, gr["gather_experts"])
        pos_t = jnp.transpose(pos, (0, 2, 1)).reshape(gr["n_req"] * seq, N_EXPERTS)
        staged.append((cap, pos_t, xg, w))
    (_, _, xg_p, w_p), (_, _, xg_s, w_s) = staged
    y_p, y_s = _ffn(xg_p, xg_s, w_p, w_s, w_gate, w_up, w_down, layer)
    outs = []
    for gr, (cap, pos_t, _, _), y in zip(groups, staged, (y_p, y_s)):
        outs.append(_combine(gr["x"], gr["mod"], pos_t, y, cap, gr["tpr"] * TM, gr["shared"], final_g))
    return outs


def kernel(x_prompt, x_sample, state_gla, cache_na_k, cache_na_v, c, c_ctx, norm1_g, norm2_g, w_mod, b_mod, conv_w_in, conv_b_in, conv_dw, conv_dw_b, conv_ln_g, conv_ln_b, conv_w_out, conv_b_out, gla_w_proj, gla_w_a1, gla_w_a2, gla_b_a, gla_norm_g, gla_w_o, na_w_qkv, na_rpb, na_w_o, router_w, moe_w_gate, moe_w_up, moe_w_down, final_g):
    n_p, seq_p, _ = x_prompt.shape
    n_s, seq_s, _ = x_sample.shape
    assert seq_p % TM == 0 and seq_s % TM == 0 and seq_s % GRID_W == 0
    tpr_p, tpr_s = seq_p // TM, seq_s // TM
    xc = x_prompt.reshape(n_p * seq_p, D)
    xs = x_sample.reshape(n_s * seq_s, D)

    n_c = 1 + n_s
    pad_c = -n_c % 8
    cvec = jnp.concatenate([c_ctx[None, :], c, jnp.zeros((pad_c, D), F32)], axis=0)
    mods = _mod_vectors(cvec, w_mod, b_mod).reshape(DEPTH, n_c + pad_c, 6, D)
    mods = jnp.pad(mods, ((0, 0), (0, 0), (0, 2), (0, 0)))
    vec = lambda a: a.reshape(1, -1)

    new_gla, new_k, new_v = [], [], []
    for i in range(DEPTH):
        kind, j = i % N_MIXERS, i // N_MIXERS
        mod_c = mods[i, 0:1]
        mod_s = mods[i, 1:1 + n_s]
        g1 = norm1_g[i].reshape(1, D)
        g2 = norm2_g[i].reshape(1, D)
        if kind == 0:
            w_in = conv_w_in[j].astype(BF16)
            w_out = conv_w_out[j].astype(BF16)
            args = (conv_dw[j], vec(conv_dw_b[j]), vec(conv_ln_g[j]), vec(conv_ln_b[j]), w_out, vec(conv_b_out[j]))
            u_c = _conv_in(xc, g1, mod_c, w_in, vec(conv_b_in[j]), tpr_p, True)
            u_s = _conv_in(xs, g1, mod_s, w_in, vec(conv_b_in[j]), tpr_s, False)
            xc = _conv_out(xc, u_c, mod_c, *args, tpr_p, True)
            xs = _conv_out(xs, u_s, mod_s, *args, tpr_s, False)
        elif kind == 1:
            w_proj = gla_w_proj[j].astype(BF16)
            w_o = gla_w_o[j].astype(BF16)
            wa1 = jnp.concatenate([gla_w_a1[j, 0], gla_w_a1[j, 1]], axis=1).astype(BF16)
            wa2 = jnp.zeros((2 * GLA_GATE_RANK, 2 * GLA_DK), F32)
            wa2 = wa2.at[:GLA_GATE_RANK, :GLA_DK].set(gla_w_a2[j, 0]).at[GLA_GATE_RANK:, GLA_DK:].set(gla_w_a2[j, 1])
            wa2 = wa2.astype(BF16)
            ba = gla_b_a[j].reshape(1, 2 * GLA_DK)
            pc = _gla_proj(xc, g1, mod_c, w_proj, wa1, wa2, ba, None, tpr_p, True)
            ps = _gla_proj(xs, g1, mod_s, w_proj, wa1, wa2, ba, _rope_tables(seq_s), tpr_s, False)
            og_c, s_ctx = _gla_core(*pc, gla_norm_g[j], None, n_p, seq_p)
            og_s, _ = _gla_core(*ps, gla_norm_g[j], state_gla[:, j], n_s, seq_s)
            new_gla.append(s_ctx)
            xc = _out_proj(xc, og_c, mod_c, w_o, tpr_p, True)
            xs = _out_proj(xs, og_s, mod_s, w_o, tpr_s, False)
        else:
            w_qkv = na_w_qkv[j].astype(BF16)
            w_o = na_w_o[j].astype(BF16)
            q_c, k_c, v_c, kf_c, vf_c = _na_qkv(xc, g1, mod_c, w_qkv, True, tpr_p, True)
            q_s, k_s, v_s = _na_qkv(xs, g1, mod_s, w_qkv, False, tpr_s, False)
            o_c = _na_ctx(q_c, k_c, v_c, n_p, seq_p)
            past = cache_na_k.shape[2]
            slabs = _na_bias_slabs(na_rpb[j], seq_s // GRID_W)
            o_s = _na_lat(q_s, k_s, v_s, cache_na_k[:, j].reshape(n_s, past, D),
                          cache_na_v[:, j].reshape(n_s, past, D), slabs, n_s, seq_s)
            new_k.append(kf_c.reshape(n_p, seq_p, NA_HEADS, NA_DH))
            new_v.append(vf_c.reshape(n_p, seq_p, NA_HEADS, NA_DH))
            xc = _out_proj(xc, o_c, mod_c, w_o, tpr_p, True)
            xs = _out_proj(xs, o_s, mod_s, w_o, tpr_s, False)

        groups = [
            dict(x=xc, mod=mod_c, n_req=n_p, tpr=tpr_p, shared=True, route_batch=n_p, gather_experts=N_EXPERTS),
            dict(x=xs, mod=mod_s, n_req=n_s, tpr=tpr_s, shared=False, route_batch=1, gather_experts=GATHER_EXPERTS),
        ]
        fg = final_g.reshape(1, D) if i == DEPTH - 1 else None
        xc, xs = _moe(groups, g2, router_w[i].T, moe_w_gate, moe_w_up, moe_w_down, i, fg)

    y_prompt = xc.reshape(n_p, seq_p, D)
    y_sample = xs.reshape(n_s, seq_s, D)
    return (y_prompt, y_sample, jnp.stack(new_gla, axis=1), jnp.stack(new_k, axis=1), jnp.stack(new_v, axis=1))
```

```python
import functools

import jax
import jax.numpy as jnp
import numpy as np
from jax import lax
from jax.experimental import pallas as pl
from jax.experimental.pallas import tpu as pltpu

F32 = jnp.float32
BF16 = jnp.bfloat16
I32 = jnp.int32

D = 1024
DEPTH = 4
GRID_W = 64
N_MIXERS = 3
CONV_WIDTH = 31
CONV_PAD = CONV_WIDTH // 2
GLA_HEADS = 4
GLA_DK = D // 2
GLA_DV = D
GLA_DKH = GLA_DK // GLA_HEADS
GLA_DVH = GLA_DV // GLA_HEADS
GLA_GATE_RANK = 16
GLA_GATE_TAU = 16.0
GLA_CHUNK = 64
NA_HEADS = 16
NA_DH = D // NA_HEADS
NA_WIN_R = 8
NA_WIN_C = 16
N_EXPERTS = 16
EXPERT_CAP_FACTOR = 2
D_EXPERT = 2 * D
ROPE_THETA = 10000.0
EPS = 1e-6

TM = 256
HALO = 16
F_TILE = 512
FFN_ROWS = 512
ROW_MERGE = 2
COMBINE_TM = 512
GATHER_EXPERTS = 4
GLA_UNROLL = 4
GLA_SCAN_UNROLL = 8
NA_GROUP = 4
NEG = -1e30
VMEM_LIMIT = 56 * 1024 * 1024
NT = (((1,), (1,)), ((), ()))
TN = (((0,), (0,)), ((), ()))


def _cparams(*sem):
    return pltpu.CompilerParams(dimension_semantics=sem, vmem_limit_bytes=VMEM_LIMIT)


def _norm_mod(x, g, shift, scale):
    ms = jnp.mean(x * x, axis=-1, keepdims=True)
    y = x * lax.rsqrt(ms + EPS) * g
    return y * (1.0 + scale) + shift


def _row_tile(tiles_per_req):
    merge = ROW_MERGE if tiles_per_req % ROW_MERGE == 0 else 1
    return TM * merge, tiles_per_req // merge


def _mod_index(tiles_per_req, shared):
    if shared:
        return lambda i: (0, 0, 0)
    return lambda i: (i // tiles_per_req, 0, 0)


def _mod_kernel(c_ref, w_ref, b_ref, o_ref):
    c = c_ref[...]
    a = (c * jax.nn.sigmoid(c)).astype(BF16)
    o_ref[0] = jnp.dot(a, w_ref[0].astype(BF16), preferred_element_type=F32) + b_ref[0]


def _mod_vectors(cvec, w_mod, b_mod):
    rows = cvec.shape[0]
    n = w_mod.shape[-1]
    tn = D
    return pl.pallas_call(
        _mod_kernel,
        out_shape=jax.ShapeDtypeStruct((DEPTH, rows, n), F32),
        grid=(DEPTH, n // tn),
        in_specs=[
            pl.BlockSpec((rows, D), lambda l, j: (0, 0)),
            pl.BlockSpec((1, D, tn), lambda l, j: (l, 0, j)),
            pl.BlockSpec((1, 1, tn), lambda l, j: (l, 0, j)),
        ],
        out_specs=pl.BlockSpec((1, rows, tn), lambda l, j: (l, 0, j)),
        compiler_params=_cparams("arbitrary", "arbitrary"),
        name="mod_vectors",
    )(cvec, w_mod, b_mod.reshape(DEPTH, 1, n))


def _out_proj_kernel(x_ref, a_ref, mod_ref, w_ref, o_ref):
    res = jnp.dot(a_ref[...], w_ref[...], preferred_element_type=F32)
    o_ref[...] = x_ref[...] + mod_ref[0, 2:3, :] * res


def _out_proj(x, a, mod, w, tiles_per_req, shared):
    t = x.shape[0]
    tm, tpr = _row_tile(tiles_per_req)
    return pl.pallas_call(
        _out_proj_kernel,
        out_shape=jax.ShapeDtypeStruct((t, D), F32),
        grid=(t // tm,),
        in_specs=[
            pl.BlockSpec((tm, D), lambda i: (i, 0)),
            pl.BlockSpec((tm, D), lambda i: (i, 0)),
            pl.BlockSpec((1, 8, D), _mod_index(tpr, shared)),
            pl.BlockSpec((D, D), lambda i: (0, 0)),
        ],
        out_specs=pl.BlockSpec((tm, D), lambda i: (i, 0)),
        compiler_params=_cparams("arbitrary"),
        name="out_proj",
    )(x, a, mod, w)


def _conv_in_kernel(x_ref, g_ref, mod_ref, w_ref, b_ref, u_ref):
    h = _norm_mod(x_ref[...], g_ref[...], mod_ref[0, 0:1, :], mod_ref[0, 1:2, :])
    ag = jnp.dot(h.astype(BF16), w_ref[...], preferred_element_type=F32) + b_ref[...]
    u_ref[...] = ag[:, :D] * jax.nn.sigmoid(ag[:, D:])


def _conv_in(x, g, mod, w_in, b_in, tiles_per_req, shared):
    t = x.shape[0]
    tm, tpr = _row_tile(tiles_per_req)
    return pl.pallas_call(
        _conv_in_kernel,
        out_shape=jax.ShapeDtypeStruct((t, D), F32),
        grid=(t // tm,),
        in_specs=[
            pl.BlockSpec((tm, D), lambda i: (i, 0)),
            pl.BlockSpec((1, D), lambda i: (0, 0)),
            pl.BlockSpec((1, 8, D), _mod_index(tpr, shared)),
            pl.BlockSpec((D, 2 * D), lambda i: (0, 0)),
            pl.BlockSpec((1, 2 * D), lambda i: (0, 0)),
        ],
        out_specs=pl.BlockSpec((tm, D), lambda i: (i, 0)),
        compiler_params=_cparams("arbitrary"),
        name="conv_in",
    )(x, g, mod, w_in, b_in)


def _conv_out_kernel(tiles_per_req, x_ref, u_ref, up_ref, un_ref, mod_ref, dw_ref, dwb_ref,
                     lng_ref, lnb_ref, w_ref, b_ref, o_ref, win_ref, cv_ref):
    k = pl.program_id(0) % tiles_per_req
    win_ref[0:HALO, :] = jnp.where(k > 0, up_ref[...], 0.0)
    win_ref[HALO:HALO + TM, :] = u_ref[...]
    win_ref[HALO + TM:, :] = jnp.where(k < tiles_per_req - 1, un_ref[...], 0.0)
    rows, lanes, sub = 64, 256, 8
    base = HALO - CONV_PAD
    span = rows + sub * ((base + CONV_WIDTH - 1) // sub)
    for c in range(TM // rows):
        for l0 in range(0, D, lanes):
            acc = jnp.zeros((rows // sub, sub, lanes), F32)
            for phase in range(sub):
                taps = [t for t in range(CONV_WIDTH) if (base + t) % sub == phase]
                if not taps:
                    continue
                xs = win_ref[c * rows + phase:c * rows + phase + span, l0:l0 + lanes]
                for t in taps:
                    a = (base + t) // sub
                    seg = xs[a * sub:a * sub + rows].reshape(rows // sub, sub, lanes)
                    acc = acc + dw_ref[t, :, l0:l0 + lanes][None] * seg
            cv_ref[c * rows:(c + 1) * rows, l0:l0 + lanes] = acc.reshape(rows, lanes)
    v = cv_ref[...] + dwb_ref[...]
    mu = jnp.mean(v, axis=-1, keepdims=True)
    vc = v - mu
    y = vc * lax.rsqrt(jnp.mean(vc * vc, axis=-1, keepdims=True) + EPS)
    y = y * lng_ref[...] + lnb_ref[...]
    y = y * jax.nn.sigmoid(y)
    res = jnp.dot(y.astype(BF16), w_ref[...], preferred_element_type=F32) + b_ref[...]
    o_ref[...] = x_ref[...] + mod_ref[0, 2:3, :] * res


def _conv_out(x, u, mod, dw, dw_b, ln_g, ln_b, w_out, b_out, tiles_per_req, shared):
    t = x.shape[0]
    hb = TM // HALO
    n_halo = t // HALO
    vec = pl.BlockSpec((1, D), lambda i: (0, 0))
    return pl.pallas_call(
        functools.partial(_conv_out_kernel, tiles_per_req),
        out_shape=jax.ShapeDtypeStruct((t, D), F32),
        grid=(t // TM,),
        in_specs=[
            pl.BlockSpec((TM, D), lambda i: (i, 0)),
            pl.BlockSpec((TM, D), lambda i: (i, 0)),
            pl.BlockSpec((HALO, D), lambda i: (jnp.maximum(i * hb - 1, 0), 0)),
            pl.BlockSpec((HALO, D), lambda i: (jnp.minimum((i + 1) * hb, n_halo - 1), 0)),
            pl.BlockSpec((1, 8, D), _mod_index(tiles_per_req, shared)),
            pl.BlockSpec((CONV_WIDTH, 8, D), lambda i: (0, 0, 0)),
            vec, vec, vec,
            pl.BlockSpec((D, D), lambda i: (0, 0)),
            vec,
        ],
        out_specs=pl.BlockSpec((TM, D), lambda i: (i, 0)),
        scratch_shapes=[pltpu.VMEM((TM + 2 * HALO, D), F32), pltpu.VMEM((TM, D), F32)],
        compiler_params=_cparams("arbitrary"),
        name="conv_out",
    )(x, u, u, u, mod, jnp.broadcast_to(dw[:, None, :], (CONV_WIDTH, 8, D)), dw_b, ln_g, ln_b, w_out, b_out)


def _rope_tables(seq):
    half = GLA_DKH // 2
    nf = half // 2
    t = jnp.arange(seq)
    row = (t // GRID_W).astype(F32)
    col = (t % GRID_W).astype(F32)
    inv = ROPE_THETA ** (-jnp.arange(nf, dtype=F32) / nf)
    ar = row[:, None] * inv[None, :]
    ac = col[:, None] * inv[None, :]
    cos = jnp.concatenate([jnp.cos(ar), jnp.cos(ar), jnp.cos(ac), jnp.cos(ac)], axis=1)
    sin = jnp.concatenate([-jnp.sin(ar), jnp.sin(ar), -jnp.sin(ac), jnp.sin(ac)], axis=1)
    return cos, sin


def _gla_proj_kernel(use_rope, *refs):
    if use_rope:
        (x_ref, g_ref, mod_ref, w_ref, wa1_ref, wa2_ref, ba_ref, cos_ref, sin_ref,
         q_ref, k_ref, v_ref, gg_ref, lgf_ref, lgb_ref) = refs
    else:
        (x_ref, g_ref, mod_ref, w_ref, wa1_ref, wa2_ref, ba_ref,
         q_ref, k_ref, v_ref, gg_ref, lgf_ref, lgb_ref) = refs
    h = _norm_mod(x_ref[...], g_ref[...], mod_ref[0, 0:1, :], mod_ref[0, 1:2, :]).astype(BF16)
    p = jnp.dot(h, w_ref[...], preferred_element_type=F32)
    nf = GLA_DKH // 4
    for ref, base, scale in ((q_ref, 0, GLA_DKH ** -0.5), (k_ref, GLA_DK, None)):
        for hd in range(GLA_HEADS):
            a = p[:, base + hd * GLA_DKH: base + (hd + 1) * GLA_DKH]
            if use_rope:
                lane = lax.broadcasted_iota(I32, a.shape, 1)
                partner = jnp.where(lane % (2 * nf) < nf,
                                    pltpu.roll(a, GLA_DKH - nf, 1), pltpu.roll(a, nf, 1))
                a = a * cos_ref[...] + partner * sin_ref[...]
            if scale is not None:
                a = a * scale
            ref[:, hd * GLA_DKH:(hd + 1) * GLA_DKH] = a
    v_ref[...] = p[:, 2 * GLA_DK: 2 * GLA_DK + GLA_DV]
    gg_ref[...] = p[:, 2 * GLA_DK + GLA_DV:]
    z1 = jnp.dot(h, wa1_ref[...], preferred_element_type=F32)
    z = jnp.dot(z1.astype(BF16), wa2_ref[...], preferred_element_type=F32) + ba_ref[...]
    lg = (jnp.minimum(z, 0.0) - jnp.log1p(jnp.exp(-jnp.abs(z)))) / GLA_GATE_TAU
    lgf_ref[...] = lg[:, :GLA_DK]
    lgb_ref[...] = lg[:, GLA_DK:]


def _gla_proj(x, g, mod, w_proj, wa1, wa2, ba, rope, tiles_per_req, shared):
    t = x.shape[0]
    tm, tpr = _row_tile(tiles_per_req)
    use_rope = rope is not None
    full = lambda a: pl.BlockSpec(a.shape, lambda i: (0,) * a.ndim)
    in_specs = [
        pl.BlockSpec((tm, D), lambda i: (i, 0)),
        pl.BlockSpec((1, D), lambda i: (0, 0)),
        pl.BlockSpec((1, 8, D), _mod_index(tpr, shared)),
        full(w_proj), full(wa1), full(wa2), full(ba),
    ]
    args = [x, g, mod, w_proj, wa1, wa2, ba]
    if use_rope:
        in_specs += [pl.BlockSpec((tm, GLA_DKH), lambda i: (i % tpr, 0))] * 2
        args += list(rope)
    row = lambda n: pl.BlockSpec((tm, n), lambda i: (i, 0))
    shp = lambda n: jax.ShapeDtypeStruct((t, n), F32)
    return pl.pallas_call(
        functools.partial(_gla_proj_kernel, use_rope),
        out_shape=(shp(GLA_DK), shp(GLA_DK), shp(GLA_DV), shp(GLA_DV), shp(GLA_DK), shp(GLA_DK)),
        grid=(t // tm,),
        in_specs=in_specs,
        out_specs=(row(GLA_DK), row(GLA_DK), row(GLA_DV), row(GLA_DV), row(GLA_DK), row(GLA_DK)),
        compiler_params=_cparams("arbitrary"),
        name="gla_proj",
    )(*args)


def _gla_core_kernel(n_chunks, has_s0, *refs):
    q_ref, k_ref, v_ref, g_ref, lgf_ref, lgb_ref, ng_ref = refs[:7]
    if has_s0:
        s0_ref, og_ref = refs[7], refs[8]
        sn_ref = None
        scratch = refs[9:]
    else:
        s0_ref = None
        og_ref, sn_ref = refs[7], refs[8]
        scratch = refs[9:]
    oacc_ref, qe_ref, dst_ref, dec_ref = scratch
    C = GLA_CHUNK
    U = GLA_UNROLL
    B = U * C
    r_i = lax.broadcasted_iota(I32, (B, B), 0)
    c_i = lax.broadcasted_iota(I32, (B, B), 1)
    same_chunk = (r_i // C) == (c_i // C)
    lg_refs = (lgf_ref, lgb_ref)
    masks = (same_chunk & (c_i <= r_i), same_chunk & (c_i >= r_i))
    tris = tuple(jnp.where(m, 1.0, 0.0).astype(BF16) for m in masks)
    tot_rows = (C - 1, 0)

    def block(m, carry):
        r0 = pl.multiple_of(m * B, B)
        q = q_ref[pl.ds(r0, B), :]
        k = k_ref[pl.ds(r0, B), :]
        v = v_ref[pl.ds(r0, B), :].astype(BF16)
        bs = []
        for d in range(2):
            lg = lg_refs[d][pl.ds(r0, B), :]
            p0 = lg.astype(BF16)
            e1 = lg - p0.astype(F32)
            p1 = e1.astype(BF16)
            p2 = (e1 - p1.astype(F32)).astype(BF16)
            bs.append(jnp.dot(tris[d], p0, preferred_element_type=F32)
                      + jnp.dot(tris[d], p1, preferred_element_type=F32)
                      + jnp.dot(tris[d], p2, preferred_element_type=F32))
        tots, qes, kes, kds = [], [], [], []
        for d in range(2):
            b = bs[d]
            tt = [b[u * C + tot_rows[d]:u * C + tot_rows[d] + 1, :] for u in range(U)]
            tot = jnp.concatenate([jnp.broadcast_to(t, (C, GLA_DKH)) for t in tt], axis=0)
            tots.append(tt)
            qes.append((q * jnp.exp(b)).astype(BF16))
            kes.append((k * jnp.exp(-b)).astype(BF16))
            kds.append((k * jnp.exp(tot - b)).astype(BF16))
        scores = [lax.dot_general(qes[d], kes[d], NT, preferred_element_type=F32) for d in range(2)]
        scores = [jnp.where(masks[d], scores[d], 0.0).astype(BF16) for d in range(2)]
        oacc_ref[pl.ds(r0, B), :] = (jnp.dot(scores[0], v, preferred_element_type=F32)
                                     + jnp.dot(scores[1], v, preferred_element_type=F32))
        for d in range(2):
            qe_ref[d, pl.ds(r0, B), :] = qes[d]
            for u in range(U):
                rows = slice(u * C, (u + 1) * C)
                dst_ref[d, m * U + u] = lax.dot_general(v[rows], kds[d][rows], TN, preferred_element_type=F32)
                dec_ref[d, pl.ds(m * U + u, 1), :] = jnp.exp(tots[d][u])
        return carry

    n_blocks = n_chunks // U
    lax.fori_loop(0, n_blocks, block, 0, unroll=2 if n_blocks % 2 == 0 else 1)

    for d in range(2):
        def scan(i, st, d=d):
            n = i if d == 0 else n_chunks - 1 - i
            r0 = pl.multiple_of(n * C, C)
            o_x = lax.dot_general(qe_ref[d, pl.ds(r0, C), :], st.astype(BF16), NT, preferred_element_type=F32)
            oacc_ref[pl.ds(r0, C), :] = oacc_ref[pl.ds(r0, C), :] + o_x
            return dec_ref[d, pl.ds(n, 1), :] * st + dst_ref[d, n]

        if has_s0:
            st0 = jnp.transpose(s0_ref[0, d, 0])
        else:
            st0 = jnp.zeros((GLA_DVH, GLA_DKH), F32)
        st = lax.fori_loop(0, n_chunks, scan, st0, unroll=min(n_chunks, GLA_SCAN_UNROLL))
        if sn_ref is not None:
            sn_ref[0, d, 0] = jnp.transpose(st)

    rows = 256

    def epilogue(i, carry):
        r0 = pl.multiple_of(i * rows, rows)
        o = oacc_ref[pl.ds(r0, rows), :]
        o = o * lax.rsqrt(jnp.mean(o * o, axis=-1, keepdims=True) + EPS)
        o = o * ng_ref[0]
        g = g_ref[pl.ds(r0, rows), :]
        og_ref[pl.ds(r0, rows), :] = (o * (g * jax.nn.sigmoid(g))).astype(BF16)
        return carry

    lax.fori_loop(0, (n_chunks * C) // rows, epilogue, 0)


def _gla_core(q, k, v, g, lgf, lgb, norm_g, s0, n_req, seq):
    t = q.shape[0]
    n_chunks = seq // GLA_CHUNK
    has_s0 = s0 is not None
    kblk = pl.BlockSpec((seq, GLA_DKH), lambda b, h: (b, h))
    vblk = pl.BlockSpec((seq, GLA_DVH), lambda b, h: (b, h))
    sblk = pl.BlockSpec((1, 2, 1, GLA_DKH, GLA_DVH), lambda b, h: (b, 0, h, 0, 0))
    in_specs = [kblk, kblk, vblk, vblk, kblk, kblk, pl.BlockSpec((1, 1, GLA_DVH), lambda b, h: (h, 0, 0))]
    args = [q, k, v, g, lgf, lgb, norm_g.reshape(GLA_HEADS, 1, GLA_DVH)]
    out_shape = [jax.ShapeDtypeStruct((t, GLA_DV), BF16)]
    out_specs = [vblk]
    if has_s0:
        in_specs.append(sblk)
        args.append(s0)
    else:
        out_shape.append(jax.ShapeDtypeStruct((n_req, 2, GLA_HEADS, GLA_DKH, GLA_DVH), F32))
        out_specs.append(sblk)
    res = pl.pallas_call(
        functools.partial(_gla_core_kernel, n_chunks, has_s0),
        out_shape=tuple(out_shape),
        grid=(n_req, GLA_HEADS),
        in_specs=in_specs,
        out_specs=tuple(out_specs),
        scratch_shapes=[
            pltpu.VMEM((seq, GLA_DVH), F32),
            pltpu.VMEM((2, seq, GLA_DKH), BF16),
            pltpu.VMEM((2, n_chunks, GLA_DVH, GLA_DKH), F32),
            pltpu.VMEM((2, max(n_chunks, 8), GLA_DKH), F32),
        ],
        compiler_params=_cparams("arbitrary", "arbitrary"),
        name="gla_core",
    )(*args)
    return (res[0], None) if has_s0 else res


def _na_qkv_kernel(emit_f32, x_ref, g_ref, mod_ref, w_ref, *outs):
    h = _norm_mod(x_ref[...], g_ref[...], mod_ref[0, 0:1, :], mod_ref[0, 1:2, :]).astype(BF16)
    qkv = jnp.dot(h, w_ref[...], preferred_element_type=F32)
    for n in range(3):
        outs[n][...] = qkv[:, n * D:(n + 1) * D].astype(BF16)
    if emit_f32:
        outs[3][...] = qkv[:, D:2 * D]
        outs[4][...] = qkv[:, 2 * D:]


def _na_qkv(x, g, mod, w_qkv, emit_f32, tiles_per_req, shared):
    t = x.shape[0]
    tm, tpr = _row_tile(tiles_per_req)
    row = pl.BlockSpec((tm, D), lambda i: (i, 0))
    n_bf, n_f = 3, (2 if emit_f32 else 0)
    return pl.pallas_call(
        functools.partial(_na_qkv_kernel, emit_f32),
        out_shape=tuple([jax.ShapeDtypeStruct((t, D), BF16)] * n_bf + [jax.ShapeDtypeStruct((t, D), F32)] * n_f),
        grid=(t // tm,),
        in_specs=[row, pl.BlockSpec((1, D), lambda i: (0, 0)),
                  pl.BlockSpec((1, 8, D), _mod_index(tpr, shared)),
                  pl.BlockSpec((D, 3 * D), lambda i: (0, 0))],
        out_specs=tuple([row] * (n_bf + n_f)),
        compiler_params=_cparams("arbitrary"),
        name="na_qkv",
    )(x, g, mod, w_qkv)


def _softmax_rows(parts):
    m = functools.reduce(jnp.maximum, [jnp.max(s, axis=-1, keepdims=True) for s in parts])
    es = [jnp.exp(s - m) for s in parts]
    den = functools.reduce(lambda a, b: a + b, [jnp.sum(e, axis=-1, keepdims=True) for e in es])
    return [(e / den).astype(BF16) for e in es]


def _na_ctx_kernel(q_ref, k_ref, v_ref, o_ref):
    seq = q_ref.shape[0]
    lane = lax.broadcasted_iota(I32, (seq, 2 * NA_DH), 1)
    lo = lane < NA_DH
    scale = NA_DH ** -0.5
    for hp in range(NA_HEADS // 2):
        sl = slice(hp * 2 * NA_DH, (hp + 1) * 2 * NA_DH)
        q, k, v = q_ref[:, sl], k_ref[:, sl], v_ref[:, sl]
        halves = []
        for half in range(2):
            keep = lo if half == 0 else jnp.logical_not(lo)
            qm = jnp.where(keep, q, jnp.zeros_like(q))
            s = lax.dot_general(qm, k, NT, preferred_element_type=F32) * scale
            (p,) = _softmax_rows([s])
            halves.append(jnp.dot(p, v, preferred_element_type=F32))
        o_ref[:, sl] = jnp.where(lo, halves[0], halves[1]).astype(BF16)


def _na_ctx(q, k, v, n_req, seq):
    blk = pl.BlockSpec((seq, D), lambda b: (b, 0))
    return pl.pallas_call(
        _na_ctx_kernel,
        out_shape=jax.ShapeDtypeStruct(q.shape, BF16),
        grid=(n_req,),
        in_specs=[blk, blk, blk],
        out_specs=blk,
        compiler_params=_cparams("arbitrary"),
        name="na_ctx",
    )(q, k, v)


def _na_bias_slabs(rpb, rows):
    wr = min(NA_WIN_R, rows)
    n_g = rows // NA_GROUP
    assert n_g >= 3 and rows % NA_GROUP == 0 and wr // 2 <= NA_GROUP and wr - wr // 2 <= NA_GROUP
    gs = np.array([0, 1, n_g - 1])
    ri = np.arange(NA_GROUP)
    ki = np.arange(3 * NA_GROUP)
    r = gs[:, None] * NA_GROUP + ri[None, :]
    r_start = np.clip(r - wr // 2, 0, rows - wr)
    krow = (gs[:, None] - 1) * NA_GROUP + ki[None, :]
    valid_r = (krow[:, None, :] >= r_start[:, :, None]) & (krow[:, None, :] < r_start[:, :, None] + wr)
    dr = np.clip(krow[:, None, :] - r[:, :, None] + (NA_WIN_R - 1), 0, 2 * NA_WIN_R - 2)
    cidx = np.arange(GRID_W)
    c_start = np.clip(cidx - NA_WIN_C // 2, 0, GRID_W - NA_WIN_C)
    col_ok = (cidx[None, :] >= c_start[:, None]) & (cidx[None, :] < c_start[:, None] + NA_WIN_C)
    dc = np.clip(cidx[None, :] - cidx[:, None] + (NA_WIN_C - 1), 0, 2 * NA_WIN_C - 2)
    n_dr, n_dc = rpb.shape[1], rpb.shape[2]
    k_pad = -n_dc % 8
    onehot = (dc.reshape(1, -1) == np.arange(n_dc + k_pad)[:, None]).astype(np.float32)
    rpb2 = jnp.pad(rpb.reshape(NA_HEADS * n_dr, n_dc).astype(F32), ((0, 0), (0, k_pad)))
    table = pl.pallas_call(
        _bias_table_kernel,
        out_shape=jax.ShapeDtypeStruct((NA_HEADS * n_dr, GRID_W * GRID_W), F32),
        name="na_bias_table",
    )(rpb2, jnp.asarray(onehot, BF16))
    table = jnp.where(col_ok, table.reshape(NA_HEADS, n_dr, GRID_W, GRID_W), NEG)
    masked = jnp.full((NA_HEADS, GRID_W, GRID_W), NEG, F32)
    cases = []
    for c in range(3):
        q_rows = []
        for a in range(NA_GROUP):
            blocks = [table[:, int(dr[c, a, b])] if valid_r[c, a, b] else masked for b in range(3 * NA_GROUP)]
            q_rows.append(jnp.concatenate(blocks, axis=2))
        cases.append(jnp.concatenate(q_rows, axis=1))
    return jnp.stack(cases, axis=0)


def _bias_table_kernel(rpb_ref, onehot_ref, t_ref):
    r = rpb_ref[...]
    p0 = r.astype(BF16)
    e1 = r - p0.astype(F32)
    p1 = e1.astype(BF16)
    p2 = (e1 - p1.astype(F32)).astype(BF16)
    oh = onehot_ref[...]
    t_ref[...] = (jnp.dot(p0, oh, preferred_element_type=F32) + jnp.dot(p1, oh, preferred_element_type=F32)
                  + jnp.dot(p2, oh, preferred_element_type=F32))


def _na_lat_kernel(q_ref, kp_ref, kc_ref, kn_ref, vp_ref, vc_ref, vn_ref, kx_ref, vx_ref, slab_ref, o_ref):
    nb, gq, _ = q_ref.shape
    lane = lax.broadcasted_iota(I32, (gq, 2 * NA_DH), 1)
    lo = lane < NA_DH
    scale = NA_DH ** -0.5
    assert scale == 2.0 ** round(np.log2(scale))
    slab = slab_ref[0].reshape(2 * gq, slab_ref.shape[-1])
    for b in range(nb):
        q = q_ref[b] * scale
        zero = jnp.zeros_like(q)
        q2 = jnp.concatenate([jnp.where(lo, q, zero), jnp.where(lo, zero, q)], axis=0)
        kcat = jnp.concatenate([kp_ref[b], kc_ref[b], kn_ref[b]], axis=0)
        vcat = jnp.concatenate([vp_ref[b], vc_ref[b], vn_ref[b]], axis=0)
        kx = kx_ref[b].astype(BF16)
        vx = vx_ref[b].astype(BF16)
        s_loc = lax.dot_general(q2, kcat, NT, preferred_element_type=F32) + slab
        s_ctx = lax.dot_general(q2, kx, NT, preferred_element_type=F32)
        m = jnp.maximum(jnp.max(s_loc, axis=-1, keepdims=True), jnp.max(s_ctx, axis=-1, keepdims=True))
        e_loc = jnp.exp(s_loc - m)
        e_ctx = jnp.exp(s_ctx - m)
        den = jnp.sum(e_loc, axis=-1, keepdims=True) + jnp.sum(e_ctx, axis=-1, keepdims=True)
        o2 = (jnp.dot(e_loc.astype(BF16), vcat, preferred_element_type=F32)
              + jnp.dot(e_ctx.astype(BF16), vx, preferred_element_type=F32)) / den
        o_ref[b] = jnp.where(lo, o2[:gq], o2[gq:]).astype(BF16)


def _na_lat(q, k, v, k_ctx, v_ctx, slabs, n_req, seq):
    gq = NA_GROUP * GRID_W
    n_g = seq // gq
    past = k_ctx.shape[1]
    q3, k3, v3 = (a.reshape(n_req, seq, D) for a in (q, k, v))
    hp_w = 2 * NA_DH
    cur = pl.BlockSpec((n_req, gq, hp_w), lambda g, hp: (0, g, hp))
    prv = pl.BlockSpec((n_req, gq, hp_w), lambda g, hp: (0, jnp.maximum(g - 1, 0), hp))
    nxt = pl.BlockSpec((n_req, gq, hp_w), lambda g, hp: (0, jnp.minimum(g + 1, n_g - 1), hp))
    ctx = pl.BlockSpec((n_req, past, hp_w), lambda g, hp: (0, 0, hp))
    case = lambda g: jnp.where(g == 0, 0, jnp.where(g == n_g - 1, 2, 1))
    slab = pl.BlockSpec((1, 2, gq, 3 * gq), lambda g, hp: (case(g), hp, 0, 0))
    out = pl.pallas_call(
        _na_lat_kernel,
        out_shape=jax.ShapeDtypeStruct((n_req, seq, D), BF16),
        grid=(n_g, NA_HEADS // 2),
        in_specs=[cur, prv, cur, nxt, prv, cur, nxt, ctx, ctx, slab],
        out_specs=cur,
        compiler_params=_cparams("arbitrary", "arbitrary"),
        name="na_lat",
    )(q3, k3, k3, k3, v3, v3, v3, k_ctx, v_ctx, slabs)
    return out.reshape(n_req * seq, D)


def _split_bf16(a):
    hi = a.astype(BF16)
    lo = (a - hi.astype(F32)).astype(BF16)
    return hi, lo


def _moe_pre_kernel(x_ref, g_ref, mod_ref, rw_ref, h_ref, aff_ref):
    h = _norm_mod(x_ref[...], g_ref[...], mod_ref[0, 3:4, :], mod_ref[0, 4:5, :])
    h_ref[...] = h.astype(BF16)
    h_hi, h_lo = _split_bf16(h)
    r_hi, r_lo = _split_bf16(rw_ref[...])
    logits = (lax.dot_general(r_hi, h_hi, NT, preferred_element_type=F32)
              + lax.dot_general(r_hi, h_lo, NT, preferred_element_type=F32)
              + lax.dot_general(r_lo, h_hi, NT, preferred_element_type=F32))
    m = jnp.max(logits, axis=0, keepdims=True)
    e = jnp.exp(logits - m)
    aff_ref[0] = e / jnp.sum(e, axis=0, keepdims=True)


def _moe_pre(x, g, mod, router_wt, n_req, tiles_per_req, shared):
    t = x.shape[0]
    seq = tiles_per_req * TM
    tm, tpr = _row_tile(tiles_per_req)
    return pl.pallas_call(
        _moe_pre_kernel,
        out_shape=(jax.ShapeDtypeStruct((t, D), BF16),
                   jax.ShapeDtypeStruct((n_req, N_EXPERTS, seq), F32)),
        grid=(t // tm,),
        in_specs=[
            pl.BlockSpec((tm, D), lambda i: (i, 0)),
            pl.BlockSpec((1, D), lambda i: (0, 0)),
            pl.BlockSpec((1, 8, D), _mod_index(tpr, shared)),
            pl.BlockSpec((N_EXPERTS, D), lambda i: (0, 0)),
        ],
        out_specs=(pl.BlockSpec((tm, D), lambda i: (i, 0)),
                   pl.BlockSpec((1, N_EXPERTS, tm), lambda i: (i // tpr, 0, i % tpr))),
        compiler_params=_cparams("arbitrary"),
        name="moe_pre",
    )(x, g, mod, router_wt)


def _route_kernel(cap, aff_ref, pos_ref):
    aff = aff_ref[...]
    nb, ne, seq = aff.shape
    aff = aff.reshape(nb * ne, seq)
    rows = nb * ne
    bits = pltpu.bitcast(aff, I32)
    thr = jnp.zeros((rows, 1), I32)
    for bit in range(30, -1, -1):
        cand = thr | (1 << bit)
        cnt = jnp.sum(jnp.where(bits >= cand, 1.0, 0.0), axis=1, keepdims=True)
        thr = jnp.where(cnt >= cap, cand, thr)
    gt = bits > thr
    eq = bits == thr
    need = cap - jnp.sum(jnp.where(gt, 1.0, 0.0), axis=1, keepdims=True)
    blk = 256
    r_i = lax.broadcasted_iota(I32, (blk, blk), 0)
    c_i = lax.broadcasted_iota(I32, (blk, blk), 1)
    upper = jnp.where(r_i < c_i, 1.0, 0.0).astype(BF16)
    off_gt = jnp.zeros((rows, 1), F32)
    off_eq = jnp.zeros((rows, 1), F32)
    for c in range(seq // blk):
        sl = slice(c * blk, (c + 1) * blk)
        g = jnp.where(gt[:, sl], 1.0, 0.0)
        q = jnp.where(eq[:, sl], 1.0, 0.0)
        cs_gt = jnp.dot(g.astype(BF16), upper, preferred_element_type=F32) + off_gt
        cs_eq = jnp.dot(q.astype(BF16), upper, preferred_element_type=F32) + off_eq
        off_gt = off_gt + jnp.sum(g, axis=1, keepdims=True)
        off_eq = off_eq + jnp.sum(q, axis=1, keepdims=True)
        sel = (g > 0.5) | ((q > 0.5) & (cs_eq < need))
        pos = cs_gt + jnp.minimum(cs_eq, need)
        pos_ref[:, :, sl] = jnp.where(sel, pos, -1.0).astype(I32).reshape(nb, ne, blk)


def _route(aff, cap, req_per_step):
    n_req, ne, seq = aff.shape
    return pl.pallas_call(
        functools.partial(_route_kernel, cap),
        out_shape=jax.ShapeDtypeStruct((n_req, ne, seq), I32),
        grid=(n_req // req_per_step,),
        in_specs=[pl.BlockSpec((req_per_step, ne, seq), lambda b: (b, 0, 0))],
        out_specs=pl.BlockSpec((req_per_step, ne, seq), lambda b: (b, 0, 0)),
        compiler_params=_cparams("arbitrary"),
        name="moe_route",
    )(aff)


def _gather_kernel(cap, experts_per_step, pos_ref, aff_ref, h_ref, xg_ref, w_ref):
    seq = h_ref.shape[0]
    h = h_ref[...]
    slot = lax.broadcasted_iota(I32, (cap, seq), 0)
    onehots = []
    for k in range(experts_per_step):
        e = pl.program_id(1) * experts_per_step + k
        pos = pos_ref[0, pl.ds(e, 1), :]
        aff = aff_ref[0, pl.ds(e, 1), :]
        hit = pos == slot
        onehots.append(jnp.where(hit, 1.0, 0.0).astype(BF16))
        w_ref[k] = jnp.sum(jnp.where(hit, aff, 0.0), axis=1, keepdims=True)
    xg = jnp.dot(jnp.concatenate(onehots, axis=0), h, preferred_element_type=F32).astype(BF16)
    xg_ref[...] = xg.reshape(experts_per_step, cap, D)


def _gather(pos, aff, h, cap, experts_per_step):
    n_req, ne, seq = pos.shape
    steps = ne // experts_per_step
    return pl.pallas_call(
        functools.partial(_gather_kernel, cap, experts_per_step),
        out_shape=(jax.ShapeDtypeStruct((ne, n_req * cap, D), BF16),
                   jax.ShapeDtypeStruct((ne, n_req * cap, 1), F32)),
        grid=(n_req, steps),
        in_specs=[
            pl.BlockSpec((1, ne, seq), lambda b, s: (b, 0, 0)),
            pl.BlockSpec((1, ne, seq), lambda b, s: (b, 0, 0)),
            pl.BlockSpec((seq, D), lambda b, s: (b, 0)),
        ],
        out_specs=(pl.BlockSpec((experts_per_step, cap, D), lambda b, s: (s, b, 0)),
                   pl.BlockSpec((experts_per_step, cap, 1), lambda b, s: (s, b, 0))),
        compiler_params=_cparams("arbitrary", "arbitrary"),
        name="moe_gather",
    )(pos, aff, h)


def _ffn_kernel(xp_ref, xs_ref, wp_ref, ws_ref, wg_ref, wu_ref, wd_ref, yp_ref, ys_ref,
                acc_ref, wgb_ref, wub_ref, wdb_ref):
    f = pl.program_id(1)
    last = pl.num_programs(1) - 1
    wgb_ref[...] = wg_ref[0, 0].astype(BF16)
    wub_ref[...] = wu_ref[0, 0].astype(BF16)
    wdb_ref[...] = wd_ref[0, 0].astype(BF16)
    n_p = xp_ref.shape[1]
    n_s = xs_ref.shape[1]
    groups = ((xp_ref, wp_ref, yp_ref, 0, n_p), (xs_ref, ws_ref, ys_ref, n_p, n_s))

    def body(first):
        for x_ref, _, _, base, n in groups:
            for r0 in range(0, n, FFN_ROWS):
                x = x_ref[0, r0:r0 + FFN_ROWS, :]
                g = jnp.dot(x, wgb_ref[...], preferred_element_type=F32)
                u = jnp.dot(x, wub_ref[...], preferred_element_type=F32)
                hid = (g * jax.nn.sigmoid(g) * u).astype(BF16)
                rows = slice(base + r0, base + r0 + FFN_ROWS)
                part = jnp.dot(hid, wdb_ref[...], preferred_element_type=F32)
                acc_ref[rows, :] = part if first else acc_ref[rows, :] + part

    pl.when(f == 0)(lambda: body(True))
    pl.when(f > 0)(lambda: body(False))

    @pl.when(f == last)
    def _():
        for _, w_ref, y_ref, base, n in groups:
            y_ref[0] = (acc_ref[base:base + n, :] * w_ref[0]).astype(BF16)


def _ffn(xg_p, xg_s, w_p, w_s, w_gate, w_up, w_down, layer):
    ne, n_p, _ = xg_p.shape
    n_s = xg_s.shape[1]
    fdim = w_gate.shape[-1]
    assert n_p % FFN_ROWS == 0 and n_s % FFN_ROWS == 0 and fdim % F_TILE == 0
    row = lambda n, last: pl.BlockSpec((1, n, last), lambda e, f: (e, 0, 0))
    return pl.pallas_call(
        _ffn_kernel,
        out_shape=(jax.ShapeDtypeStruct((ne, n_p, D), BF16), jax.ShapeDtypeStruct((ne, n_s, D), BF16)),
        grid=(ne, fdim // F_TILE),
        in_specs=[
            row(n_p, D), row(n_s, D), row(n_p, 1), row(n_s, 1),
            pl.BlockSpec((1, 1, D, F_TILE), lambda e, f: (layer, e, 0, f)),
            pl.BlockSpec((1, 1, D, F_TILE), lambda e, f: (layer, e, 0, f)),
            pl.BlockSpec((1, 1, F_TILE, D), lambda e, f: (layer, e, f, 0)),
        ],
        out_specs=(row(n_p, D), row(n_s, D)),
        scratch_shapes=[
            pltpu.VMEM((n_p + n_s, D), F32),
            pltpu.VMEM((D, F_TILE), BF16),
            pltpu.VMEM((D, F_TILE), BF16),
            pltpu.VMEM((F_TILE, D), BF16),
        ],
        compiler_params=_cparams("arbitrary", "arbitrary"),
        name="moe_ffn",
    )(xg_p, xg_s, w_p, w_s, w_gate, w_up, w_down)


def _combine_kernel(cap, x_ref, mod_ref, post_ref, y_ref, *rest):
    fg_ref, o_ref = rest if len(rest) == 2 else (None, rest[0])
    ne = y_ref.shape[0]
    tm = x_ref.shape[0]
    post = post_ref[...]
    if cap % 128 == 0:
        lane = lax.broadcasted_iota(I32, (tm, cap), 1)
        hit = jnp.concatenate(
            [jnp.where(post[:, e:e + 1] == lane, 1.0, 0.0).astype(BF16) for e in range(ne)], axis=1)
    else:
        lane = lax.broadcasted_iota(I32, (tm, ne * cap), 1)
        hit = jnp.zeros((tm, ne * cap), F32)
        for e in range(ne):
            col = post[:, e:e + 1]
            hit = hit + jnp.where((col >= 0) & (col + e * cap == lane), 1.0, 0.0)
        hit = hit.astype(BF16)
    y = y_ref[...].reshape(ne * cap, D)
    res = jnp.dot(hit, y, preferred_element_type=F32)
    out = x_ref[...] + mod_ref[0, 5:6, :] * res
    if fg_ref is not None:
        out = out * lax.rsqrt(jnp.mean(out * out, axis=-1, keepdims=True) + EPS) * fg_ref[...]
    o_ref[...] = out


def _combine(x, mod, pos_t, y, cap, seq, shared, final_g):
    t = x.shape[0]
    ne = y.shape[0]
    tm = min(seq, COMBINE_TM)
    tiles_per_req = seq // tm
    in_specs = [
        pl.BlockSpec((tm, D), lambda i: (i, 0)),
        pl.BlockSpec((1, 8, D), _mod_index(tiles_per_req, shared)),
        pl.BlockSpec((tm, ne), lambda i: (i, 0)),
        pl.BlockSpec((ne, cap, D), lambda i: (0, i // tiles_per_req, 0)),
    ]
    args = [x, mod, pos_t, y]
    if final_g is not None:
        in_specs.append(pl.BlockSpec((1, D), lambda i: (0, 0)))
        args.append(final_g)
    return pl.pallas_call(
        functools.partial(_combine_kernel, cap),
        out_shape=jax.ShapeDtypeStruct((t, D), F32),
        grid=(t // tm,),
        in_specs=in_specs,
        out_specs=pl.BlockSpec((tm, D), lambda i: (i, 0)),
        compiler_params=_cparams("arbitrary"),
        name="moe_combine",
    )(*args)


def _moe(groups, g2, router_wt, w_gate, w_up, w_down, layer, final_g=None):
    staged = []
    for gr in groups:
        seq = gr["tpr"] * TM
        cap = EXPERT_CAP_FACTOR * seq // N_EXPERTS
        h, aff = _moe_pre(gr["x"], g2, gr["mod"], router_wt, gr["n_req"], gr["tpr"], gr["shared"])
        pos = _route(aff, cap, gr["route_batch"])
        xg, w = _gather(pos, aff, h, cap, gr["gather_experts"])
        pos_t = jnp.transpose(pos, (0, 2, 1)).reshape(gr["n_req"] * seq, N_EXPERTS)
        staged.append((cap, pos_t, xg, w))
    (_, _, xg_p, w_p), (_, _, xg_s, w_s) = staged
    y_p, y_s = _ffn(xg_p, xg_s, w_p, w_s, w_gate, w_up, w_down, layer)
    outs = []
    for gr, (cap, pos_t, _, _), y in zip(groups, staged, (y_p, y_s)):
        outs.append(_combine(gr["x"], gr["mod"], pos_t, y, cap, gr["tpr"] * TM, gr["shared"], final_g))
    return outs


def kernel(x_prompt, x_sample, state_gla, cache_na_k, cache_na_v, c, c_ctx, norm1_g, norm2_g, w_mod, b_mod, conv_w_in, conv_b_in, conv_dw, conv_dw_b, conv_ln_g, conv_ln_b, conv_w_out, conv_b_out, gla_w_proj, gla_w_a1, gla_w_a2, gla_b_a, gla_norm_g, gla_w_o, na_w_qkv, na_rpb, na_w_o, router_w, moe_w_gate, moe_w_up, moe_w_down, final_g):
    n_p, seq_p, _ = x_prompt.shape
    n_s, seq_s, _ = x_sample.shape
    assert seq_p % TM == 0 and seq_s % TM == 0 and seq_s % GRID_W == 0
    tpr_p, tpr_s = seq_p // TM, seq_s // TM
    xc = x_prompt.reshape(n_p * seq_p, D)
    xs = x_sample.reshape(n_s * seq_s, D)

    n_c = 1 + n_s
    pad_c = -n_c % 8
    cvec = jnp.concatenate([c_ctx[None, :], c, jnp.zeros((pad_c, D), F32)], axis=0)
    mods = _mod_vectors(cvec, w_mod, b_mod).reshape(DEPTH, n_c + pad_c, 6, D)
    mods = jnp.pad(mods, ((0, 0), (0, 0), (0, 2), (0, 0)))
    vec = lambda a: a.reshape(1, -1)

    new_gla, new_k, new_v = [], [], []
    for i in range(DEPTH):
        kind, j = i % N_MIXERS, i // N_MIXERS
        mod_c = mods[i, 0:1]
        mod_s = mods[i, 1:1 + n_s]
        g1 = norm1_g[i].reshape(1, D)
        g2 = norm2_g[i].reshape(1, D)
        if kind == 0:
            w_in = conv_w_in[j].astype(BF16)
            w_out = conv_w_out[j].astype(BF16)
            args = (conv_dw[j], vec(conv_dw_b[j]), vec(conv_ln_g[j]), vec(conv_ln_b[j]), w_out, vec(conv_b_out[j]))
            u_c = _conv_in(xc, g1, mod_c, w_in, vec(conv_b_in[j]), tpr_p, True)
            u_s = _conv_in(xs, g1, mod_s, w_in, vec(conv_b_in[j]), tpr_s, False)
            xc = _conv_out(xc, u_c, mod_c, *args, tpr_p, True)
            xs = _conv_out(xs, u_s, mod_s, *args, tpr_s, False)
        elif kind == 1:
            w_proj = gla_w_proj[j].astype(BF16)
            w_o = gla_w_o[j].astype(BF16)
            wa1 = jnp.concatenate([gla_w_a1[j, 0], gla_w_a1[j, 1]], axis=1).astype(BF16)
            wa2 = jnp.zeros((2 * GLA_GATE_RANK, 2 * GLA_DK), F32)
            wa2 = wa2.at[:GLA_GATE_RANK, :GLA_DK].set(gla_w_a2[j, 0]).at[GLA_GATE_RANK:, GLA_DK:].set(gla_w_a2[j, 1])
            wa2 = wa2.astype(BF16)
            ba = gla_b_a[j].reshape(1, 2 * GLA_DK)
            pc = _gla_proj(xc, g1, mod_c, w_proj, wa1, wa2, ba, None, tpr_p, True)
            ps = _gla_proj(xs, g1, mod_s, w_proj, wa1, wa2, ba, _rope_tables(seq_s), tpr_s, False)
            og_c, s_ctx = _gla_core(*pc, gla_norm_g[j], None, n_p, seq_p)
            og_s, _ = _gla_core(*ps, gla_norm_g[j], state_gla[:, j], n_s, seq_s)
            new_gla.append(s_ctx)
            xc = _out_proj(xc, og_c, mod_c, w_o, tpr_p, True)
            xs = _out_proj(xs, og_s, mod_s, w_o, tpr_s, False)
        else:
            w_qkv = na_w_qkv[j].astype(BF16)
            w_o = na_w_o[j].astype(BF16)
            q_c, k_c, v_c, kf_c, vf_c = _na_qkv(xc, g1, mod_c, w_qkv, True, tpr_p, True)
            q_s, k_s, v_s = _na_qkv(xs, g1, mod_s, w_qkv, False, tpr_s, False)
            o_c = _na_ctx(q_c, k_c, v_c, n_p, seq_p)
            past = cache_na_k.shape[2]
            slabs = _na_bias_slabs(na_rpb[j], seq_s // GRID_W)
            o_s = _na_lat(q_s, k_s, v_s, cache_na_k[:, j].reshape(n_s, past, D),
                          cache_na_v[:, j].reshape(n_s, past, D), slabs, n_s, seq_s)
            new_k.append(kf_c.reshape(n_p, seq_p, NA_HEADS, NA_DH))
            new_v.append(vf_c.reshape(n_p, seq_p, NA_HEADS, NA_DH))
            xc = _out_proj(xc, o_c, mod_c, w_o, tpr_p, True)
            xs = _out_proj(xs, o_s, mod_s, w_o, tpr_s, False)

        groups = [
            dict(x=xc, mod=mod_c, n_req=n_p, tpr=tpr_p, shared=True, route_batch=n_p, gather_experts=N_EXPERTS),
            dict(x=xs, mod=mod_s, n_req=n_s, tpr=tpr_s, shared=False, route_batch=2 if n_s % 2 == 0 else 1,
                 gather_experts=GATHER_EXPERTS),
        ]
        fg = final_g.reshape(1, D) if i == DEPTH - 1 else None
        xc, xs = _moe(groups, g2, router_w[i].T, moe_w_gate, moe_w_up, moe_w_down, i, fg)

    y_prompt = xc.reshape(n_p, seq_p, D)
    y_sample = xs.reshape(n_s, seq_s, D)
    return (y_prompt, y_sample, jnp.stack(new_gla, axis=1), jnp.stack(new_k, axis=1), jnp.stack(new_v, axis=1))
```

```python
import functools

import jax
import jax.numpy as jnp
import numpy as np
from jax import lax
from jax.experimental import pallas as pl
from jax.experimental.pallas import tpu as pltpu

F32 = jnp.float32
BF16 = jnp.bfloat16
I32 = jnp.int32

D = 1024
DEPTH = 4
GRID_W = 64
N_MIXERS = 3
CONV_WIDTH = 31
CONV_PAD = CONV_WIDTH // 2
GLA_HEADS = 4
GLA_DK = D // 2
GLA_DV = D
GLA_DKH = GLA_DK // GLA_HEADS
GLA_DVH = GLA_DV // GLA_HEADS
GLA_GATE_RANK = 16
GLA_GATE_TAU = 16.0
GLA_CHUNK = 64
NA_HEADS = 16
NA_DH = D // NA_HEADS
NA_WIN_R = 8
NA_WIN_C = 16
N_EXPERTS = 16
EXPERT_CAP_FACTOR = 2
D_EXPERT = 2 * D
ROPE_THETA = 10000.0
EPS = 1e-6

TM = 256
HALO = 16
F_TILE = 512
FFN_ROWS = 512
ROW_MERGE = 2
COMBINE_TM = 512
GATHER_EXPERTS = 4
GLA_UNROLL = 4
GLA_SCAN_UNROLL = 8
NA_GROUP = 4
NEG = -1e30
VMEM_LIMIT = 56 * 1024 * 1024
NT = (((1,), (1,)), ((), ()))
TN = (((0,), (0,)), ((), ()))


def _cparams(*sem):
    return pltpu.CompilerParams(dimension_semantics=sem, vmem_limit_bytes=VMEM_LIMIT)


def _norm_mod(x, g, shift, scale):
    ms = jnp.mean(x * x, axis=-1, keepdims=True)
    y = x * lax.rsqrt(ms + EPS) * g
    return y * (1.0 + scale) + shift


def _row_tile(tiles_per_req):
    merge = ROW_MERGE if tiles_per_req % ROW_MERGE == 0 else 1
    return TM * merge, tiles_per_req // merge


def _mod_index(tiles_per_req, shared):
    if shared:
        return lambda i: (0, 0, 0)
    return lambda i: (i // tiles_per_req, 0, 0)


def _mod_kernel(c_ref, w_ref, b_ref, o_ref):
    c = c_ref[...]
    a = (c * jax.nn.sigmoid(c)).astype(BF16)
    o_ref[0] = jnp.dot(a, w_ref[0].astype(BF16), preferred_element_type=F32) + b_ref[0]


def _mod_vectors(cvec, w_mod, b_mod):
    rows = cvec.shape[0]
    n = w_mod.shape[-1]
    tn = D
    return pl.pallas_call(
        _mod_kernel,
        out_shape=jax.ShapeDtypeStruct((DEPTH, rows, n), F32),
        grid=(DEPTH, n // tn),
        in_specs=[
            pl.BlockSpec((rows, D), lambda l, j: (0, 0)),
            pl.BlockSpec((1, D, tn), lambda l, j: (l, 0, j)),
            pl.BlockSpec((1, 1, tn), lambda l, j: (l, 0, j)),
        ],
        out_specs=pl.BlockSpec((1, rows, tn), lambda l, j: (l, 0, j)),
        compiler_params=_cparams("arbitrary", "arbitrary"),
        name="mod_vectors",
    )(cvec, w_mod, b_mod.reshape(DEPTH, 1, n))


def _out_proj_kernel(x_ref, a_ref, mod_ref, w_ref, o_ref):
    res = jnp.dot(a_ref[...], w_ref[...], preferred_element_type=F32)
    o_ref[...] = x_ref[...] + mod_ref[0, 2:3, :] * res


def _out_proj(x, a, mod, w, tiles_per_req, shared):
    t = x.shape[0]
    tm, tpr = _row_tile(tiles_per_req)
    return pl.pallas_call(
        _out_proj_kernel,
        out_shape=jax.ShapeDtypeStruct((t, D), F32),
        grid=(t // tm,),
        in_specs=[
            pl.BlockSpec((tm, D), lambda i: (i, 0)),
            pl.BlockSpec((tm, D), lambda i: (i, 0)),
            pl.BlockSpec((1, 8, D), _mod_index(tpr, shared)),
            pl.BlockSpec((D, D), lambda i: (0, 0)),
        ],
        out_specs=pl.BlockSpec((tm, D), lambda i: (i, 0)),
        compiler_params=_cparams("arbitrary"),
        name="out_proj",
    )(x, a, mod, w)


def _conv_in_kernel(x_ref, g_ref, mod_ref, w_ref, b_ref, u_ref):
    h = _norm_mod(x_ref[...], g_ref[...], mod_ref[0, 0:1, :], mod_ref[0, 1:2, :])
    ag = jnp.dot(h.astype(BF16), w_ref[...], preferred_element_type=F32) + b_ref[...]
    u_ref[...] = ag[:, :D] * jax.nn.sigmoid(ag[:, D:])


def _conv_in(x, g, mod, w_in, b_in, tiles_per_req, shared):
    t = x.shape[0]
    tm, tpr = _row_tile(tiles_per_req)
    return pl.pallas_call(
        _conv_in_kernel,
        out_shape=jax.ShapeDtypeStruct((t, D), F32),
        grid=(t // tm,),
        in_specs=[
            pl.BlockSpec((tm, D), lambda i: (i, 0)),
            pl.BlockSpec((1, D), lambda i: (0, 0)),
            pl.BlockSpec((1, 8, D), _mod_index(tpr, shared)),
            pl.BlockSpec((D, 2 * D), lambda i: (0, 0)),
            pl.BlockSpec((1, 2 * D), lambda i: (0, 0)),
        ],
        out_specs=pl.BlockSpec((tm, D), lambda i: (i, 0)),
        compiler_params=_cparams("arbitrary"),
        name="conv_in",
    )(x, g, mod, w_in, b_in)


def _conv_out_kernel(tiles_per_req, x_ref, u_ref, up_ref, un_ref, mod_ref, dw_ref, dwb_ref,
                     lng_ref, lnb_ref, w_ref, b_ref, o_ref, win_ref, cv_ref):
    k = pl.program_id(0) % tiles_per_req
    win_ref[0:HALO, :] = jnp.where(k > 0, up_ref[...], 0.0)
    win_ref[HALO:HALO + TM, :] = u_ref[...]
    win_ref[HALO + TM:, :] = jnp.where(k < tiles_per_req - 1, un_ref[...], 0.0)
    rows, lanes, sub = 64, 256, 8
    base = HALO - CONV_PAD
    span = rows + sub * ((base + CONV_WIDTH - 1) // sub)
    for c in range(TM // rows):
        for l0 in range(0, D, lanes):
            acc = jnp.zeros((rows // sub, sub, lanes), F32)
            for phase in range(sub):
                taps = [t for t in range(CONV_WIDTH) if (base + t) % sub == phase]
                if not taps:
                    continue
                xs = win_ref[c * rows + phase:c * rows + phase + span, l0:l0 + lanes]
                for t in taps:
                    a = (base + t) // sub
                    seg = xs[a * sub:a * sub + rows].reshape(rows // sub, sub, lanes)
                    acc = acc + dw_ref[t, :, l0:l0 + lanes][None] * seg
            cv_ref[c * rows:(c + 1) * rows, l0:l0 + lanes] = acc.reshape(rows, lanes)
    v = cv_ref[...] + dwb_ref[...]
    mu = jnp.mean(v, axis=-1, keepdims=True)
    vc = v - mu
    y = vc * lax.rsqrt(jnp.mean(vc * vc, axis=-1, keepdims=True) + EPS)
    y = y * lng_ref[...] + lnb_ref[...]
    y = y * jax.nn.sigmoid(y)
    res = jnp.dot(y.astype(BF16), w_ref[...], preferred_element_type=F32) + b_ref[...]
    o_ref[...] = x_ref[...] + mod_ref[0, 2:3, :] * res


def _conv_out(x, u, mod, dw, dw_b, ln_g, ln_b, w_out, b_out, tiles_per_req, shared):
    t = x.shape[0]
    hb = TM // HALO
    n_halo = t // HALO
    vec = pl.BlockSpec((1, D), lambda i: (0, 0))
    return pl.pallas_call(
        functools.partial(_conv_out_kernel, tiles_per_req),
        out_shape=jax.ShapeDtypeStruct((t, D), F32),
        grid=(t // TM,),
        in_specs=[
            pl.BlockSpec((TM, D), lambda i: (i, 0)),
            pl.BlockSpec((TM, D), lambda i: (i, 0)),
            pl.BlockSpec((HALO, D), lambda i: (jnp.maximum(i * hb - 1, 0), 0)),
            pl.BlockSpec((HALO, D), lambda i: (jnp.minimum((i + 1) * hb, n_halo - 1), 0)),
            pl.BlockSpec((1, 8, D), _mod_index(tiles_per_req, shared)),
            pl.BlockSpec((CONV_WIDTH, 8, D), lambda i: (0, 0, 0)),
            vec, vec, vec,
            pl.BlockSpec((D, D), lambda i: (0, 0)),
            vec,
        ],
        out_specs=pl.BlockSpec((TM, D), lambda i: (i, 0)),
        scratch_shapes=[pltpu.VMEM((TM + 2 * HALO, D), F32), pltpu.VMEM((TM, D), F32)],
        compiler_params=_cparams("arbitrary"),
        name="conv_out",
    )(x, u, u, u, mod, jnp.broadcast_to(dw[:, None, :], (CONV_WIDTH, 8, D)), dw_b, ln_g, ln_b, w_out, b_out)


def _rope_tables(seq):
    half = GLA_DKH // 2
    nf = half // 2
    t = jnp.arange(seq)
    row = (t // GRID_W).astype(F32)
    col = (t % GRID_W).astype(F32)
    inv = ROPE_THETA ** (-jnp.arange(nf, dtype=F32) / nf)
    ar = row[:, None] * inv[None, :]
    ac = col[:, None] * inv[None, :]
    cos = jnp.concatenate([jnp.cos(ar), jnp.cos(ar), jnp.cos(ac), jnp.cos(ac)], axis=1)
    sin = jnp.concatenate([-jnp.sin(ar), jnp.sin(ar), -jnp.sin(ac), jnp.sin(ac)], axis=1)
    return cos, sin


def _gla_proj_kernel(use_rope, *refs):
    if use_rope:
        (x_ref, g_ref, mod_ref, w_ref, wa1_ref, wa2_ref, ba_ref, cos_ref, sin_ref,
         q_ref, k_ref, v_ref, gg_ref, lgf_ref, lgb_ref) = refs
    else:
        (x_ref, g_ref, mod_ref, w_ref, wa1_ref, wa2_ref, ba_ref,
         q_ref, k_ref, v_ref, gg_ref, lgf_ref, lgb_ref) = refs
    h = _norm_mod(x_ref[...], g_ref[...], mod_ref[0, 0:1, :], mod_ref[0, 1:2, :]).astype(BF16)
    p = jnp.dot(h, w_ref[...], preferred_element_type=F32)
    nf = GLA_DKH // 4
    for ref, base, scale in ((q_ref, 0, GLA_DKH ** -0.5), (k_ref, GLA_DK, None)):
        for hd in range(GLA_HEADS):
            a = p[:, base + hd * GLA_DKH: base + (hd + 1) * GLA_DKH]
            if use_rope:
                lane = lax.broadcasted_iota(I32, a.shape, 1)
                partner = jnp.where(lane % (2 * nf) < nf,
                                    pltpu.roll(a, GLA_DKH - nf, 1), pltpu.roll(a, nf, 1))
                a = a * cos_ref[...] + partner * sin_ref[...]
            if scale is not None:
                a = a * scale
            ref[:, hd * GLA_DKH:(hd + 1) * GLA_DKH] = a
    v_ref[...] = p[:, 2 * GLA_DK: 2 * GLA_DK + GLA_DV]
    gg_ref[...] = p[:, 2 * GLA_DK + GLA_DV:]
    z1 = jnp.dot(h, wa1_ref[...], preferred_element_type=F32)
    z = jnp.dot(z1.astype(BF16), wa2_ref[...], preferred_element_type=F32) + ba_ref[...]
    lg = (jnp.minimum(z, 0.0) - jnp.log1p(jnp.exp(-jnp.abs(z)))) / GLA_GATE_TAU
    lgf_ref[...] = lg[:, :GLA_DK]
    lgb_ref[...] = lg[:, GLA_DK:]


def _gla_proj(x, g, mod, w_proj, wa1, wa2, ba, rope, tiles_per_req, shared):
    t = x.shape[0]
    tm, tpr = _row_tile(tiles_per_req)
    use_rope = rope is not None
    full = lambda a: pl.BlockSpec(a.shape, lambda i: (0,) * a.ndim)
    in_specs = [
        pl.BlockSpec((tm, D), lambda i: (i, 0)),
        pl.BlockSpec((1, D), lambda i: (0, 0)),
        pl.BlockSpec((1, 8, D), _mod_index(tpr, shared)),
        full(w_proj), full(wa1), full(wa2), full(ba),
    ]
    args = [x, g, mod, w_proj, wa1, wa2, ba]
    if use_rope:
        in_specs += [pl.BlockSpec((tm, GLA_DKH), lambda i: (i % tpr, 0))] * 2
        args += list(rope)
    row = lambda n: pl.BlockSpec((tm, n), lambda i: (i, 0))
    shp = lambda n: jax.ShapeDtypeStruct((t, n), F32)
    return pl.pallas_call(
        functools.partial(_gla_proj_kernel, use_rope),
        out_shape=(shp(GLA_DK), shp(GLA_DK), shp(GLA_DV), shp(GLA_DV), shp(GLA_DK), shp(GLA_DK)),
        grid=(t // tm,),
        in_specs=in_specs,
        out_specs=(row(GLA_DK), row(GLA_DK), row(GLA_DV), row(GLA_DV), row(GLA_DK), row(GLA_DK)),
        compiler_params=_cparams("arbitrary"),
        name="gla_proj",
    )(*args)


def _gla_core_kernel(n_chunks, has_s0, *refs):
    q_ref, k_ref, v_ref, g_ref, lgf_ref, lgb_ref, ng_ref = refs[:7]
    if has_s0:
        s0_ref, og_ref = refs[7], refs[8]
        sn_ref = None
        scratch = refs[9:]
    else:
        s0_ref = None
        og_ref, sn_ref = refs[7], refs[8]
        scratch = refs[9:]
    oacc_ref, qe_ref, dst_ref, dec_ref = scratch
    C = GLA_CHUNK
    U = GLA_UNROLL
    B = U * C
    r_i = lax.broadcasted_iota(I32, (B, B), 0)
    c_i = lax.broadcasted_iota(I32, (B, B), 1)
    same_chunk = (r_i // C) == (c_i // C)
    lg_refs = (lgf_ref, lgb_ref)
    masks = (same_chunk & (c_i <= r_i), same_chunk & (c_i >= r_i))
    tris = tuple(jnp.where(m, 1.0, 0.0).astype(BF16) for m in masks)
    tot_rows = (C - 1, 0)

    def block(m, carry):
        r0 = pl.multiple_of(m * B, B)
        q = q_ref[pl.ds(r0, B), :]
        k = k_ref[pl.ds(r0, B), :]
        v = v_ref[pl.ds(r0, B), :].astype(BF16)
        bs = []
        for d in range(2):
            lg = lg_refs[d][pl.ds(r0, B), :]
            p0 = lg.astype(BF16)
            e1 = lg - p0.astype(F32)
            p1 = e1.astype(BF16)
            p2 = (e1 - p1.astype(F32)).astype(BF16)
            bs.append(jnp.dot(tris[d], p0, preferred_element_type=F32)
                      + jnp.dot(tris[d], p1, preferred_element_type=F32)
                      + jnp.dot(tris[d], p2, preferred_element_type=F32))
        tots, qes, kes, kds = [], [], [], []
        for d in range(2):
            b = bs[d]
            tt = [b[u * C + tot_rows[d]:u * C + tot_rows[d] + 1, :] for u in range(U)]
            tot = jnp.concatenate([jnp.broadcast_to(t, (C, GLA_DKH)) for t in tt], axis=0)
            tots.append(tt)
            qes.append((q * jnp.exp(b)).astype(BF16))
            kes.append((k * jnp.exp(-b)).astype(BF16))
            kds.append((k * jnp.exp(tot - b)).astype(BF16))
        scores = [lax.dot_general(qes[d], kes[d], NT, preferred_element_type=F32) for d in range(2)]
        scores = [jnp.where(masks[d], scores[d], 0.0).astype(BF16) for d in range(2)]
        oacc_ref[pl.ds(r0, B), :] = (jnp.dot(scores[0], v, preferred_element_type=F32)
                                     + jnp.dot(scores[1], v, preferred_element_type=F32))
        for d in range(2):
            qe_ref[d, pl.ds(r0, B), :] = qes[d]
            for u in range(U):
                rows = slice(u * C, (u + 1) * C)
                dst_ref[d, m * U + u] = lax.dot_general(v[rows], kds[d][rows], TN, preferred_element_type=F32)
                dec_ref[d, pl.ds(m * U + u, 1), :] = jnp.exp(tots[d][u])
        return carry

    n_blocks = n_chunks // U
    lax.fori_loop(0, n_blocks, block, 0, unroll=2 if n_blocks % 2 == 0 else 1)

    for d in range(2):
        def scan(i, st, d=d):
            n = i if d == 0 else n_chunks - 1 - i
            r0 = pl.multiple_of(n * C, C)
            o_x = lax.dot_general(qe_ref[d, pl.ds(r0, C), :], st.astype(BF16), NT, preferred_element_type=F32)
            oacc_ref[pl.ds(r0, C), :] = oacc_ref[pl.ds(r0, C), :] + o_x
            return dec_ref[d, pl.ds(n, 1), :] * st + dst_ref[d, n]

        if has_s0:
            st0 = jnp.transpose(s0_ref[0, d, 0])
        else:
            st0 = jnp.zeros((GLA_DVH, GLA_DKH), F32)
        st = lax.fori_loop(0, n_chunks, scan, st0, unroll=min(n_chunks, GLA_SCAN_UNROLL))
        if sn_ref is not None:
            sn_ref[0, d, 0] = jnp.transpose(st)

    rows = 256

    def epilogue(i, carry):
        r0 = pl.multiple_of(i * rows, rows)
        o = oacc_ref[pl.ds(r0, rows), :]
        o = o * lax.rsqrt(jnp.mean(o * o, axis=-1, keepdims=True) + EPS)
        o = o * ng_ref[0]
        g = g_ref[pl.ds(r0, rows), :]
        og_ref[pl.ds(r0, rows), :] = (o * (g * jax.nn.sigmoid(g))).astype(BF16)
        return carry

    lax.fori_loop(0, (n_chunks * C) // rows, epilogue, 0)


def _gla_core(q, k, v, g, lgf, lgb, norm_g, s0, n_req, seq):
    t = q.shape[0]
    n_chunks = seq // GLA_CHUNK
    has_s0 = s0 is not None
    kblk = pl.BlockSpec((seq, GLA_DKH), lambda b, h: (b, h))
    vblk = pl.BlockSpec((seq, GLA_DVH), lambda b, h: (b, h))
    sblk = pl.BlockSpec((1, 2, 1, GLA_DKH, GLA_DVH), lambda b, h: (b, 0, h, 0, 0))
    in_specs = [kblk, kblk, vblk, vblk, kblk, kblk, pl.BlockSpec((1, 1, GLA_DVH), lambda b, h: (h, 0, 0))]
    args = [q, k, v, g, lgf, lgb, norm_g.reshape(GLA_HEADS, 1, GLA_DVH)]
    out_shape = [jax.ShapeDtypeStruct((t, GLA_DV), BF16)]
    out_specs = [vblk]
    if has_s0:
        in_specs.append(sblk)
        args.append(s0)
    else:
        out_shape.append(jax.ShapeDtypeStruct((n_req, 2, GLA_HEADS, GLA_DKH, GLA_DVH), F32))
        out_specs.append(sblk)
    res = pl.pallas_call(
        functools.partial(_gla_core_kernel, n_chunks, has_s0),
        out_shape=tuple(out_shape),
        grid=(n_req, GLA_HEADS),
        in_specs=in_specs,
        out_specs=tuple(out_specs),
        scratch_shapes=[
            pltpu.VMEM((seq, GLA_DVH), F32),
            pltpu.VMEM((2, seq, GLA_DKH), BF16),
            pltpu.VMEM((2, n_chunks, GLA_DVH, GLA_DKH), F32),
            pltpu.VMEM((2, max(n_chunks, 8), GLA_DKH), F32),
        ],
        compiler_params=_cparams("arbitrary", "arbitrary"),
        name="gla_core",
    )(*args)
    return (res[0], None) if has_s0 else res


def _na_qkv_kernel(emit_f32, x_ref, g_ref, mod_ref, w_ref, *outs):
    h = _norm_mod(x_ref[...], g_ref[...], mod_ref[0, 0:1, :], mod_ref[0, 1:2, :]).astype(BF16)
    qkv = jnp.dot(h, w_ref[...], preferred_element_type=F32)
    for n in range(3):
        outs[n][...] = qkv[:, n * D:(n + 1) * D].astype(BF16)
    if emit_f32:
        outs[3][...] = qkv[:, D:2 * D]
        outs[4][...] = qkv[:, 2 * D:]


def _na_qkv(x, g, mod, w_qkv, emit_f32, tiles_per_req, shared):
    t = x.shape[0]
    tm, tpr = _row_tile(tiles_per_req)
    row = pl.BlockSpec((tm, D), lambda i: (i, 0))
    n_bf, n_f = 3, (2 if emit_f32 else 0)
    return pl.pallas_call(
        functools.partial(_na_qkv_kernel, emit_f32),
        out_shape=tuple([jax.ShapeDtypeStruct((t, D), BF16)] * n_bf + [jax.ShapeDtypeStruct((t, D), F32)] * n_f),
        grid=(t // tm,),
        in_specs=[row, pl.BlockSpec((1, D), lambda i: (0, 0)),
                  pl.BlockSpec((1, 8, D), _mod_index(tpr, shared)),
                  pl.BlockSpec((D, 3 * D), lambda i: (0, 0))],
        out_specs=tuple([row] * (n_bf + n_f)),
        compiler_params=_cparams("arbitrary"),
        name="na_qkv",
    )(x, g, mod, w_qkv)


def _softmax_rows(parts):
    m = functools.reduce(jnp.maximum, [jnp.max(s, axis=-1, keepdims=True) for s in parts])
    es = [jnp.exp(s - m) for s in parts]
    den = functools.reduce(lambda a, b: a + b, [jnp.sum(e, axis=-1, keepdims=True) for e in es])
    return [(e / den).astype(BF16) for e in es]


def _na_ctx_kernel(q_ref, k_ref, v_ref, o_ref):
    seq = q_ref.shape[0]
    lane = lax.broadcasted_iota(I32, (seq, 2 * NA_DH), 1)
    lo = lane < NA_DH
    scale = NA_DH ** -0.5
    for hp in range(NA_HEADS // 2):
        sl = slice(hp * 2 * NA_DH, (hp + 1) * 2 * NA_DH)
        q, k, v = q_ref[:, sl], k_ref[:, sl], v_ref[:, sl]
        halves = []
        for half in range(2):
            keep = lo if half == 0 else jnp.logical_not(lo)
            qm = jnp.where(keep, q, jnp.zeros_like(q))
            s = lax.dot_general(qm, k, NT, preferred_element_type=F32) * scale
            (p,) = _softmax_rows([s])
            halves.append(jnp.dot(p, v, preferred_element_type=F32))
        o_ref[:, sl] = jnp.where(lo, halves[0], halves[1]).astype(BF16)


def _na_ctx(q, k, v, n_req, seq):
    blk = pl.BlockSpec((seq, D), lambda b: (b, 0))
    return pl.pallas_call(
        _na_ctx_kernel,
        out_shape=jax.ShapeDtypeStruct(q.shape, BF16),
        grid=(n_req,),
        in_specs=[blk, blk, blk],
        out_specs=blk,
        compiler_params=_cparams("arbitrary"),
        name="na_ctx",
    )(q, k, v)


def _na_bias_slabs(rpb, rows):
    wr = min(NA_WIN_R, rows)
    n_g = rows // NA_GROUP
    assert n_g >= 3 and rows % NA_GROUP == 0 and wr // 2 <= NA_GROUP and wr - wr // 2 <= NA_GROUP
    gs = np.array([0, 1, n_g - 1])
    ri = np.arange(NA_GROUP)
    ki = np.arange(3 * NA_GROUP)
    r = gs[:, None] * NA_GROUP + ri[None, :]
    r_start = np.clip(r - wr // 2, 0, rows - wr)
    krow = (gs[:, None] - 1) * NA_GROUP + ki[None, :]
    valid_r = (krow[:, None, :] >= r_start[:, :, None]) & (krow[:, None, :] < r_start[:, :, None] + wr)
    dr = np.clip(krow[:, None, :] - r[:, :, None] + (NA_WIN_R - 1), 0, 2 * NA_WIN_R - 2)
    cidx = np.arange(GRID_W)
    c_start = np.clip(cidx - NA_WIN_C // 2, 0, GRID_W - NA_WIN_C)
    col_ok = (cidx[None, :] >= c_start[:, None]) & (cidx[None, :] < c_start[:, None] + NA_WIN_C)
    dc = np.clip(cidx[None, :] - cidx[:, None] + (NA_WIN_C - 1), 0, 2 * NA_WIN_C - 2)
    n_dr, n_dc = rpb.shape[1], rpb.shape[2]
    k_pad = -n_dc % 8
    onehot = (dc.reshape(1, -1) == np.arange(n_dc + k_pad)[:, None]).astype(np.float32)
    rpb2 = jnp.pad(rpb.reshape(NA_HEADS * n_dr, n_dc).astype(F32), ((0, 0), (0, k_pad)))
    table = pl.pallas_call(
        _bias_table_kernel,
        out_shape=jax.ShapeDtypeStruct((NA_HEADS * n_dr, GRID_W * GRID_W), F32),
        name="na_bias_table",
    )(rpb2, jnp.asarray(onehot, BF16))
    table = jnp.where(col_ok, table.reshape(NA_HEADS, n_dr, GRID_W, GRID_W), NEG)
    masked = jnp.full((NA_HEADS, GRID_W, GRID_W), NEG, F32)
    cases = []
    for c in range(3):
        q_rows = []
        for a in range(NA_GROUP):
            blocks = [table[:, int(dr[c, a, b])] if valid_r[c, a, b] else masked for b in range(3 * NA_GROUP)]
            q_rows.append(jnp.concatenate(blocks, axis=2))
        cases.append(jnp.concatenate(q_rows, axis=1))
    return jnp.stack(cases, axis=0)


def _bias_table_kernel(rpb_ref, onehot_ref, t_ref):
    r = rpb_ref[...]
    p0 = r.astype(BF16)
    e1 = r - p0.astype(F32)
    p1 = e1.astype(BF16)
    p2 = (e1 - p1.astype(F32)).astype(BF16)
    oh = onehot_ref[...]
    t_ref[...] = (jnp.dot(p0, oh, preferred_element_type=F32) + jnp.dot(p1, oh, preferred_element_type=F32)
                  + jnp.dot(p2, oh, preferred_element_type=F32))


def _na_lat_kernel(q_ref, kp_ref, kc_ref, kn_ref, vp_ref, vc_ref, vn_ref, kx_ref, vx_ref, slab_ref, o_ref):
    nb, gq, _ = q_ref.shape
    lane = lax.broadcasted_iota(I32, (gq, 2 * NA_DH), 1)
    lo = lane < NA_DH
    scale = NA_DH ** -0.5
    assert scale == 2.0 ** round(np.log2(scale))
    slab = slab_ref[0].reshape(2 * gq, slab_ref.shape[-1])
    for b in range(nb):
        q = q_ref[b] * scale
        zero = jnp.zeros_like(q)
        q2 = jnp.concatenate([jnp.where(lo, q, zero), jnp.where(lo, zero, q)], axis=0)
        kcat = jnp.concatenate([kp_ref[b], kc_ref[b], kn_ref[b]], axis=0)
        vcat = jnp.concatenate([vp_ref[b], vc_ref[b], vn_ref[b]], axis=0)
        kx = kx_ref[b].astype(BF16)
        vx = vx_ref[b].astype(BF16)
        s_loc = lax.dot_general(q2, kcat, NT, preferred_element_type=F32) + slab
        s_ctx = lax.dot_general(q2, kx, NT, preferred_element_type=F32)
        m = jnp.maximum(jnp.max(s_loc, axis=-1, keepdims=True), jnp.max(s_ctx, axis=-1, keepdims=True))
        e_loc = jnp.exp(s_loc - m)
        e_ctx = jnp.exp(s_ctx - m)
        den = jnp.sum(e_loc, axis=-1, keepdims=True) + jnp.sum(e_ctx, axis=-1, keepdims=True)
        o2 = (jnp.dot(e_loc.astype(BF16), vcat, preferred_element_type=F32)
              + jnp.dot(e_ctx.astype(BF16), vx, preferred_element_type=F32)) / den
        o_ref[b] = jnp.where(lo, o2[:gq], o2[gq:]).astype(BF16)


def _na_lat(q, k, v, k_ctx, v_ctx, slabs, n_req, seq):
    gq = NA_GROUP * GRID_W
    n_g = seq // gq
    past = k_ctx.shape[1]
    q3, k3, v3 = (a.reshape(n_req, seq, D) for a in (q, k, v))
    hp_w = 2 * NA_DH
    cur = pl.BlockSpec((n_req, gq, hp_w), lambda g, hp: (0, g, hp))
    prv = pl.BlockSpec((n_req, gq, hp_w), lambda g, hp: (0, jnp.maximum(g - 1, 0), hp))
    nxt = pl.BlockSpec((n_req, gq, hp_w), lambda g, hp: (0, jnp.minimum(g + 1, n_g - 1), hp))
    ctx = pl.BlockSpec((n_req, past, hp_w), lambda g, hp: (0, 0, hp))
    case = lambda g: jnp.where(g == 0, 0, jnp.where(g == n_g - 1, 2, 1))
    slab = pl.BlockSpec((1, 2, gq, 3 * gq), lambda g, hp: (case(g), hp, 0, 0))
    out = pl.pallas_call(
        _na_lat_kernel,
        out_shape=jax.ShapeDtypeStruct((n_req, seq, D), BF16),
        grid=(n_g, NA_HEADS // 2),
        in_specs=[cur, prv, cur, nxt, prv, cur, nxt, ctx, ctx, slab],
        out_specs=cur,
        compiler_params=_cparams("arbitrary", "arbitrary"),
        name="na_lat",
    )(q3, k3, k3, k3, v3, v3, v3, k_ctx, v_ctx, slabs)
    return out.reshape(n_req * seq, D)


def _split_bf16(a):
    hi = a.astype(BF16)
    lo = (a - hi.astype(F32)).astype(BF16)
    return hi, lo


def _moe_pre_kernel(x_ref, g_ref, mod_ref, rw_ref, h_ref, aff_ref):
    h = _norm_mod(x_ref[...], g_ref[...], mod_ref[0, 3:4, :], mod_ref[0, 4:5, :])
    h_ref[...] = h.astype(BF16)
    h_hi, h_lo = _split_bf16(h)
    r_hi, r_lo = _split_bf16(rw_ref[...])
    logits = (jnp.dot(h_hi, r_hi, preferred_element_type=F32)
              + jnp.dot(h_lo, r_hi, preferred_element_type=F32)
              + jnp.dot(h_hi, r_lo, preferred_element_type=F32))
    m = jnp.max(logits, axis=-1, keepdims=True)
    e = jnp.exp(logits - m)
    aff_ref[...] = e / jnp.sum(e, axis=-1, keepdims=True)


def _moe_pre(x, g, mod, router_w, n_req, tiles_per_req, shared):
    t = x.shape[0]
    seq = tiles_per_req * TM
    tm, tpr = _row_tile(tiles_per_req)
    h, aff = pl.pallas_call(
        _moe_pre_kernel,
        out_shape=(jax.ShapeDtypeStruct((t, D), BF16),
                   jax.ShapeDtypeStruct((t, N_EXPERTS), F32)),
        grid=(t // tm,),
        in_specs=[
            pl.BlockSpec((tm, D), lambda i: (i, 0)),
            pl.BlockSpec((1, D), lambda i: (0, 0)),
            pl.BlockSpec((1, 8, D), _mod_index(tpr, shared)),
            pl.BlockSpec((D, N_EXPERTS), lambda i: (0, 0)),
        ],
        out_specs=(pl.BlockSpec((tm, D), lambda i: (i, 0)),
                   pl.BlockSpec((tm, N_EXPERTS), lambda i: (i, 0))),
        compiler_params=_cparams("arbitrary"),
        name="moe_pre",
    )(x, g, mod, router_w)
    return h, jnp.transpose(aff.reshape(n_req, seq, N_EXPERTS), (0, 2, 1))


def _route_kernel(cap, aff_ref, pos_ref):
    aff = aff_ref[...]
    nb, ne, seq = aff.shape
    aff = aff.reshape(nb * ne, seq)
    rows = nb * ne
    bits = pltpu.bitcast(aff, I32)
    thr = jnp.zeros((rows, 1), I32)
    for bit in range(30, -1, -1):
        cand = thr | (1 << bit)
        cnt = jnp.sum(jnp.where(bits >= cand, 1.0, 0.0), axis=1, keepdims=True)
        thr = jnp.where(cnt >= cap, cand, thr)
    gt = bits > thr
    eq = bits == thr
    need = cap - jnp.sum(jnp.where(gt, 1.0, 0.0), axis=1, keepdims=True)
    blk = 256
    r_i = lax.broadcasted_iota(I32, (blk, blk), 0)
    c_i = lax.broadcasted_iota(I32, (blk, blk), 1)
    upper = jnp.where(r_i < c_i, 1.0, 0.0).astype(BF16)
    off_gt = jnp.zeros((rows, 1), F32)
    off_eq = jnp.zeros((rows, 1), F32)
    for c in range(seq // blk):
        sl = slice(c * blk, (c + 1) * blk)
        g = jnp.where(gt[:, sl], 1.0, 0.0)
        q = jnp.where(eq[:, sl], 1.0, 0.0)
        cs_gt = jnp.dot(g.astype(BF16), upper, preferred_element_type=F32) + off_gt
        cs_eq = jnp.dot(q.astype(BF16), upper, preferred_element_type=F32) + off_eq
        off_gt = off_gt + jnp.sum(g, axis=1, keepdims=True)
        off_eq = off_eq + jnp.sum(q, axis=1, keepdims=True)
        sel = (g > 0.5) | ((q > 0.5) & (cs_eq < need))
        pos = cs_gt + jnp.minimum(cs_eq, need)
        pos_ref[:, :, sl] = jnp.where(sel, pos, -1.0).astype(I32).reshape(nb, ne, blk)


def _route(aff, cap, req_per_step):
    n_req, ne, seq = aff.shape
    return pl.pallas_call(
        functools.partial(_route_kernel, cap),
        out_shape=jax.ShapeDtypeStruct((n_req, ne, seq), I32),
        grid=(n_req // req_per_step,),
        in_specs=[pl.BlockSpec((req_per_step, ne, seq), lambda b: (b, 0, 0))],
        out_specs=pl.BlockSpec((req_per_step, ne, seq), lambda b: (b, 0, 0)),
        compiler_params=_cparams("arbitrary"),
        name="moe_route",
    )(aff)


def _gather_kernel(cap, experts_per_step, pos_ref, aff_ref, h_ref, xg_ref, w_ref):
    seq = h_ref.shape[0]
    h = h_ref[...]
    slot = lax.broadcasted_iota(I32, (cap, seq), 0)
    onehots = []
    for k in range(experts_per_step):
        e = pl.program_id(1) * experts_per_step + k
        pos = pos_ref[0, pl.ds(e, 1), :]
        aff = aff_ref[0, pl.ds(e, 1), :]
        hit = pos == slot
        onehots.append(jnp.where(hit, 1.0, 0.0).astype(BF16))
        w_ref[k] = jnp.sum(jnp.where(hit, aff, 0.0), axis=1, keepdims=True)
    xg = jnp.dot(jnp.concatenate(onehots, axis=0), h, preferred_element_type=F32).astype(BF16)
    xg_ref[...] = xg.reshape(experts_per_step, cap, D)


def _gather(pos, aff, h, cap, experts_per_step):
    n_req, ne, seq = pos.shape
    steps = ne // experts_per_step
    return pl.pallas_call(
        functools.partial(_gather_kernel, cap, experts_per_step),
        out_shape=(jax.ShapeDtypeStruct((ne, n_req * cap, D), BF16),
                   jax.ShapeDtypeStruct((ne, n_req * cap, 1), F32)),
        grid=(n_req, steps),
        in_specs=[
            pl.BlockSpec((1, ne, seq), lambda b, s: (b, 0, 0)),
            pl.BlockSpec((1, ne, seq), lambda b, s: (b, 0, 0)),
            pl.BlockSpec((seq, D), lambda b, s: (b, 0)),
        ],
        out_specs=(pl.BlockSpec((experts_per_step, cap, D), lambda b, s: (s, b, 0)),
                   pl.BlockSpec((experts_per_step, cap, 1), lambda b, s: (s, b, 0))),
        compiler_params=_cparams("arbitrary", "arbitrary"),
        name="moe_gather",
    )(pos, aff, h)


def _ffn_kernel(xp_ref, xs_ref, wp_ref, ws_ref, wg_ref, wu_ref, wd_ref, yp_ref, ys_ref,
                acc_ref, wgb_ref, wub_ref, wdb_ref):
    f = pl.program_id(1)
    last = pl.num_programs(1) - 1
    wgb_ref[...] = wg_ref[0, 0].astype(BF16)
    wub_ref[...] = wu_ref[0, 0].astype(BF16)
    wdb_ref[...] = wd_ref[0, 0].astype(BF16)
    n_p = xp_ref.shape[1]
    n_s = xs_ref.shape[1]
    groups = ((xp_ref, wp_ref, yp_ref, 0, n_p), (xs_ref, ws_ref, ys_ref, n_p, n_s))

    def body(first):
        for x_ref, _, _, base, n in groups:
            for r0 in range(0, n, FFN_ROWS):
                x = x_ref[0, r0:r0 + FFN_ROWS, :]
                g = jnp.dot(x, wgb_ref[...], preferred_element_type=F32)
                u = jnp.dot(x, wub_ref[...], preferred_element_type=F32)
                hid = (g * jax.nn.sigmoid(g) * u).astype(BF16)
                rows = slice(base + r0, base + r0 + FFN_ROWS)
                part = jnp.dot(hid, wdb_ref[...], preferred_element_type=F32)
                acc_ref[rows, :] = part if first else acc_ref[rows, :] + part

    pl.when(f == 0)(lambda: body(True))
    pl.when(f > 0)(lambda: body(False))

    @pl.when(f == last)
    def _():
        for _, w_ref, y_ref, base, n in groups:
            y_ref[0] = (acc_ref[base:base + n, :] * w_ref[0]).astype(BF16)


def _ffn(xg_p, xg_s, w_p, w_s, w_gate, w_up, w_down, layer):
    ne, n_p, _ = xg_p.shape
    n_s = xg_s.shape[1]
    fdim = w_gate.shape[-1]
    assert n_p % FFN_ROWS == 0 and n_s % FFN_ROWS == 0 and fdim % F_TILE == 0
    row = lambda n, last: pl.BlockSpec((1, n, last), lambda e, f: (e, 0, 0))
    return pl.pallas_call(
        _ffn_kernel,
        out_shape=(jax.ShapeDtypeStruct((ne, n_p, D), BF16), jax.ShapeDtypeStruct((ne, n_s, D), BF16)),
        grid=(ne, fdim // F_TILE),
        in_specs=[
            row(n_p, D), row(n_s, D), row(n_p, 1), row(n_s, 1),
            pl.BlockSpec((1, 1, D, F_TILE), lambda e, f: (layer, e, 0, f)),
            pl.BlockSpec((1, 1, D, F_TILE), lambda e, f: (layer, e, 0, f)),
            pl.BlockSpec((1, 1, F_TILE, D), lambda e, f: (layer, e, f, 0)),
        ],
        out_specs=(row(n_p, D), row(n_s, D)),
        scratch_shapes=[
            pltpu.VMEM((n_p + n_s, D), F32),
            pltpu.VMEM((D, F_TILE), BF16),
            pltpu.VMEM((D, F_TILE), BF16),
            pltpu.VMEM((F_TILE, D), BF16),
        ],
        compiler_params=_cparams("arbitrary", "arbitrary"),
        name="moe_ffn",
    )(xg_p, xg_s, w_p, w_s, w_gate, w_up, w_down)


def _combine_kernel(cap, x_ref, mod_ref, post_ref, y_ref, *rest):
    fg_ref, o_ref = rest if len(rest) == 2 else (None, rest[0])
    ne = y_ref.shape[0]
    tm = x_ref.shape[0]
    post = post_ref[...]
    if cap % 128 == 0:
        lane = lax.broadcasted_iota(I32, (tm, cap), 1)
        hit = jnp.concatenate(
            [jnp.where(post[:, e:e + 1] == lane, 1.0, 0.0).astype(BF16) for e in range(ne)], axis=1)
    else:
        lane = lax.broadcasted_iota(I32, (tm, ne * cap), 1)
        hit = jnp.zeros((tm, ne * cap), F32)
        for e in range(ne):
            col = post[:, e:e + 1]
            hit = hit + jnp.where((col >= 0) & (col + e * cap == lane), 1.0, 0.0)
        hit = hit.astype(BF16)
    y = y_ref[...].reshape(ne * cap, D)
    res = jnp.dot(hit, y, preferred_element_type=F32)
    out = x_ref[...] + mod_ref[0, 5:6, :] * res
    if fg_ref is not None:
        out = out * lax.rsqrt(jnp.mean(out * out, axis=-1, keepdims=True) + EPS) * fg_ref[...]
    o_ref[...] = out


def _combine(x, mod, pos_t, y, cap, seq, shared, final_g):
    t = x.shape[0]
    ne = y.shape[0]
    tm = min(seq, COMBINE_TM)
    tiles_per_req = seq // tm
    in_specs = [
        pl.BlockSpec((tm, D), lambda i: (i, 0)),
        pl.BlockSpec((1, 8, D), _mod_index(tiles_per_req, shared)),
        pl.BlockSpec((tm, ne), lambda i: (i, 0)),
        pl.BlockSpec((ne, cap, D), lambda i: (0, i // tiles_per_req, 0)),
    ]
    args = [x, mod, pos_t, y]
    if final_g is not None:
        in_specs.append(pl.BlockSpec((1, D), lambda i: (0, 0)))
        args.append(final_g)
    return pl.pallas_call(
        functools.partial(_combine_kernel, cap),
        out_shape=jax.ShapeDtypeStruct((t, D), F32),
        grid=(t // tm,),
        in_specs=in_specs,
        out_specs=pl.BlockSpec((tm, D), lambda i: (i, 0)),
        compiler_params=_cparams("arbitrary"),
        name="moe_combine",
    )(*args)


def _moe(groups, g2, router_wt, w_gate, w_up, w_down, layer, final_g=None):
    staged = []
    for gr in groups:
        seq = gr["tpr"] * TM
        cap = EXPERT_CAP_FACTOR * seq // N_EXPERTS
        h, aff = _moe_pre(gr["x"], g2, gr["mod"], router_wt, gr["n_req"], gr["tpr"], gr["shared"])
        pos = _route(aff, cap, gr["route_batch"])
        xg, w = _gather(pos, aff, h, cap, gr["gather_experts"])
        pos_t = jnp.transpose(pos, (0, 2, 1)).reshape(gr["n_req"] * seq, N_EXPERTS)
        staged.append((cap, pos_t, xg, w))
    (_, _, xg_p, w_p), (_, _, xg_s, w_s) = staged
    y_p, y_s = _ffn(xg_p, xg_s, w_p, w_s, w_gate, w_up, w_down, layer)
    outs = []
    for gr, (cap, pos_t, _, _), y in zip(groups, staged, (y_p, y_s)):
        outs.append(_combine(gr["x"], gr["mod"], pos_t, y, cap, gr["tpr"] * TM, gr["shared"], final_g))
    return outs


def kernel(x_prompt, x_sample, state_gla, cache_na_k, cache_na_v, c, c_ctx, norm1_g, norm2_g, w_mod, b_mod, conv_w_in, conv_b_in, conv_dw, conv_dw_b, conv_ln_g, conv_ln_b, conv_w_out, conv_b_out, gla_w_proj, gla_w_a1, gla_w_a2, gla_b_a, gla_norm_g, gla_w_o, na_w_qkv, na_rpb, na_w_o, router_w, moe_w_gate, moe_w_up, moe_w_down, final_g):
    n_p, seq_p, _ = x_prompt.shape
    n_s, seq_s, _ = x_sample.shape
    assert seq_p % TM == 0 and seq_s % TM == 0 and seq_s % GRID_W == 0
    tpr_p, tpr_s = seq_p // TM, seq_s // TM
    xc = x_prompt.reshape(n_p * seq_p, D)
    xs = x_sample.reshape(n_s * seq_s, D)

    n_c = 1 + n_s
    pad_c = -n_c % 8
    cvec = jnp.concatenate([c_ctx[None, :], c, jnp.zeros((pad_c, D), F32)], axis=0)
    mods = _mod_vectors(cvec, w_mod, b_mod).reshape(DEPTH, n_c + pad_c, 6, D)
    mods = jnp.pad(mods, ((0, 0), (0, 0), (0, 2), (0, 0)))
    vec = lambda a: a.reshape(1, -1)

    new_gla, new_k, new_v = [], [], []
    for i in range(DEPTH):
        kind, j = i % N_MIXERS, i // N_MIXERS
        mod_c = mods[i, 0:1]
        mod_s = mods[i, 1:1 + n_s]
        g1 = norm1_g[i].reshape(1, D)
        g2 = norm2_g[i].reshape(1, D)
        if kind == 0:
            w_in = conv_w_in[j].astype(BF16)
            w_out = conv_w_out[j].astype(BF16)
            args = (conv_dw[j], vec(conv_dw_b[j]), vec(conv_ln_g[j]), vec(conv_ln_b[j]), w_out, vec(conv_b_out[j]))
            u_c = _conv_in(xc, g1, mod_c, w_in, vec(conv_b_in[j]), tpr_p, True)
            u_s = _conv_in(xs, g1, mod_s, w_in, vec(conv_b_in[j]), tpr_s, False)
            xc = _conv_out(xc, u_c, mod_c, *args, tpr_p, True)
            xs = _conv_out(xs, u_s, mod_s, *args, tpr_s, False)
        elif kind == 1:
            w_proj = gla_w_proj[j].astype(BF16)
            w_o = gla_w_o[j].astype(BF16)
            wa1 = jnp.concatenate([gla_w_a1[j, 0], gla_w_a1[j, 1]], axis=1).astype(BF16)
            wa2 = jnp.zeros((2 * GLA_GATE_RANK, 2 * GLA_DK), F32)
            wa2 = wa2.at[:GLA_GATE_RANK, :GLA_DK].set(gla_w_a2[j, 0]).at[GLA_GATE_RANK:, GLA_DK:].set(gla_w_a2[j, 1])
            wa2 = wa2.astype(BF16)
            ba = gla_b_a[j].reshape(1, 2 * GLA_DK)
            pc = _gla_proj(xc, g1, mod_c, w_proj, wa1, wa2, ba, None, tpr_p, True)
            ps = _gla_proj(xs, g1, mod_s, w_proj, wa1, wa2, ba, _rope_tables(seq_s), tpr_s, False)
            og_c, s_ctx = _gla_core(*pc, gla_norm_g[j], None, n_p, seq_p)
            og_s, _ = _gla_core(*ps, gla_norm_g[j], state_gla[:, j], n_s, seq_s)
            new_gla.append(s_ctx)
            xc = _out_proj(xc, og_c, mod_c, w_o, tpr_p, True)
            xs = _out_proj(xs, og_s, mod_s, w_o, tpr_s, False)
        else:
            w_qkv = na_w_qkv[j].astype(BF16)
            w_o = na_w_o[j].astype(BF16)
            q_c, k_c, v_c, kf_c, vf_c = _na_qkv(xc, g1, mod_c, w_qkv, True, tpr_p, True)
            q_s, k_s, v_s = _na_qkv(xs, g1, mod_s, w_qkv, False, tpr_s, False)
            o_c = _na_ctx(q_c, k_c, v_c, n_p, seq_p)
            past = cache_na_k.shape[2]
            slabs = _na_bias_slabs(na_rpb[j], seq_s // GRID_W)
            o_s = _na_lat(q_s, k_s, v_s, cache_na_k[:, j].reshape(n_s, past, D),
                          cache_na_v[:, j].reshape(n_s, past, D), slabs, n_s, seq_s)
            new_k.append(kf_c.reshape(n_p, seq_p, NA_HEADS, NA_DH))
            new_v.append(vf_c.reshape(n_p, seq_p, NA_HEADS, NA_DH))
            xc = _out_proj(xc, o_c, mod_c, w_o, tpr_p, True)
            xs = _out_proj(xs, o_s, mod_s, w_o, tpr_s, False)

        groups = [
            dict(x=xc, mod=mod_c, n_req=n_p, tpr=tpr_p, shared=True, route_batch=n_p, gather_experts=N_EXPERTS),
            dict(x=xs, mod=mod_s, n_req=n_s, tpr=tpr_s, shared=False, route_batch=2 if n_s % 2 == 0 else 1,
                 gather_experts=GATHER_EXPERTS),
        ]
        fg = final_g.reshape(1, D) if i == DEPTH - 1 else None
        xc, xs = _moe(groups, g2, router_w[i], moe_w_gate, moe_w_up, moe_w_down, i, fg)

    y_prompt = xc.reshape(n_p, seq_p, D)
    y_sample = xs.reshape(n_s, seq_s, D)
    return (y_prompt, y_sample, jnp.stack(new_gla, axis=1), jnp.stack(new_k, axis=1), jnp.stack(new_v, axis=1))
```
